```python
import math
import jax, jax.numpy as jnp
from jax import lax
import numpy as np

D_MODEL = 1024
BATCH = 8
SEQ = 2048
DEPTH = 1

GRID_W = 64
CTX_LEN = 256
D_CONV = 512
CONV_K = 3
MLA_HEADS = 8
QK_NOPE = 64
QK_ROPE = 32
V_DIM = 64
D_MLA = MLA_HEADS * V_DIM
Q_LORA = 256
KV_LORA = 128
ROPE_THETA = 10000.0
Q_BLOCK = 128
N_BRANCH = 2
LN_EPS = 1e-5
RMS_EPS = 1e-6
DEEPNORM_ALPHA = (2.0 * DEPTH) ** 0.25
DEEPNORM_BETA = (8.0 * DEPTH) ** -0.25
N_IN = 4 * D_CONV + Q_LORA + KV_LORA + QK_ROPE + D_MLA + N_BRANCH * D_MODEL

kernel_name = "hybrid_conv_mla_prefix_dit_block"


def _layer_norm(t, g, b):
    tf = t.astype(jnp.float32)
    mu = jnp.mean(tf, axis=-1, keepdims=True)
    var = jnp.mean(jnp.square(tf - mu), axis=-1, keepdims=True)
    y = (tf - mu) * lax.rsqrt(var + LN_EPS)
    return (y * g.astype(jnp.float32) + b.astype(jnp.float32)).astype(t.dtype)


def _rms_norm(t, g):
    tf = t.astype(jnp.float32)
    y = tf * lax.rsqrt(jnp.mean(jnp.square(tf), axis=-1, keepdims=True) + RMS_EPS)
    return (y * g.astype(jnp.float32)).astype(t.dtype)


def _split_proj(p):
    sizes = (D_CONV, D_CONV, D_CONV, D_CONV, Q_LORA, KV_LORA, QK_ROPE, D_MLA, D_MODEL, D_MODEL)
    idx = [int(v) for v in np.cumsum(sizes)[:-1]]
    return jnp.split(p, idx, axis=-1)


def _axial_rope_tables(n):
    rows = n // GRID_W
    row_pos = jnp.broadcast_to(jnp.arange(rows)[:, None], (rows, GRID_W)).reshape(-1).astype(jnp.float32)
    col_pos = jnp.broadcast_to(jnp.arange(GRID_W)[None, :], (rows, GRID_W)).reshape(-1).astype(jnp.float32)
    axis_dim = QK_ROPE // 2
    inv_freq = ROPE_THETA ** (-jnp.arange(0, axis_dim, 2, dtype=jnp.float32) / axis_dim)
    ang_r = row_pos[:, None] * inv_freq[None, :]
    ang_c = col_pos[:, None] * inv_freq[None, :]
    return (jnp.cos(ang_r), jnp.sin(ang_r), jnp.cos(ang_c), jnp.sin(ang_c))


def _rotate(t, cos, sin):
    n = cos.shape[0]
    shp = (n,) + (1,) * (t.ndim - 3) + (cos.shape[-1],)
    cos = cos.reshape(shp).astype(t.dtype)
    sin = sin.reshape(shp).astype(t.dtype)
    h = t.shape[-1] // 2
    t1, t2 = t[..., :h], t[..., h:]
    return jnp.concatenate([t1 * cos - t2 * sin, t2 * cos + t1 * sin], axis=-1)


def _axial_rope(t, tables):
    cr, sr, cc, sc = tables
    h = t.shape[-1] // 2
    return jnp.concatenate([_rotate(t[..., :h], cr, sr), _rotate(t[..., h:], cc, sc)], axis=-1)


def _short_conv(u, w):
    n = u.shape[1]
    up = jnp.pad(u, ((0, 0), (1, 1), (0, 0)))
    return up[:, :n] * w[0] + up[:, 1:n + 1] * w[1] + up[:, 2:] * w[2]


def _mla_q(cq, q_norm_g, w_uq):
    q = _rms_norm(cq, q_norm_g) @ w_uq
    q = q.reshape(cq.shape[:-1] + (MLA_HEADS, QK_NOPE + QK_ROPE))
    return q[..., :QK_NOPE], q[..., QK_NOPE:]


def _mla_kv(ckv, kv_norm_g, w_ukv):
    kv = _rms_norm(ckv, kv_norm_g) @ w_ukv
    kv = kv.reshape(ckv.shape[:-1] + (MLA_HEADS, QK_NOPE + V_DIM))
    return kv[..., :QK_NOPE], kv[..., QK_NOPE:]


def _mla_attend(qn, qr, kn, kr, v):
    scale = (QK_NOPE + QK_ROPE) ** -0.5
    s = jnp.einsum('bqhd,bkhd->bhqk', qn, kn) + jnp.einsum('bqhd,bkd->bhqk', qr, kr)
    p = jax.nn.softmax(s.astype(jnp.float32) * scale, axis=-1).astype(v.dtype)
    return jnp.einsum('bhqk,bkhd->bqhd', p, v)


def _mla_attend_blocked(qn, qr, kn, kr, v):
    b, n = qn.shape[:2]
    nb = n // Q_BLOCK

    def to_blocks(t):
        return t.reshape((b, nb, Q_BLOCK) + t.shape[2:]).swapaxes(0, 1)

    out = lax.map(lambda qb: _mla_attend(qb[0], qb[1], kn, kr, v), (to_blocks(qn), to_blocks(qr)))
    return out.swapaxes(0, 1).reshape(b, n, MLA_HEADS, V_DIM)


def _mixer_out(parts, att, conv_w, w_out_conv, w_out_mla, w_o):
    xc, bc, cc, gc, _, _, _, gm, g_conv, g_mla = parts
    y_conv = (jax.nn.silu(gc) * bc * _short_conv(cc * xc, conv_w)) @ w_out_conv
    y_mla = (jax.nn.silu(gm) * att.reshape(att.shape[:2] + (D_MLA,))) @ w_out_mla
    merged = jax.nn.sigmoid(g_conv) * y_conv + jax.nn.sigmoid(g_mla) * y_mla
    return merged @ w_o


def _hybrid_layer(x, ctx, c_silu, cctx_silu, rope_tables, w_ada, b_ada, w_in, conv_w, q_norm_g, w_uq,
                  kv_norm_g, w_ukv, w_out_conv, w_out_mla, w_o, ln_g, ln_b, update_ctx):
    shift_x, scale_x, gate_x = jnp.split(c_silu @ w_ada + b_ada, 3, axis=-1)
    shift_c, scale_c, gate_c = jnp.split(cctx_silu @ w_ada + b_ada, 3, axis=-1)
    hx = x * (1 + scale_x[:, None, :]) + shift_x[:, None, :]
    hc = ctx * (1 + scale_c) + shift_c
    px = _split_proj(hx @ w_in)
    pc = _split_proj(hc @ w_in)

    kn_c, v_c = _mla_kv(pc[5], kv_norm_g, w_ukv)
    kr_c = pc[6]
    kn_x, v_x = _mla_kv(px[5], kv_norm_g, w_ukv)
    kr_x = _axial_rope(px[6], rope_tables)
    qn_x, qr_x = _mla_q(px[4], q_norm_g, w_uq)
    qr_x = _axial_rope(qr_x, rope_tables)
    kn_all = jnp.concatenate([kn_c, kn_x], axis=1)
    kr_all = jnp.concatenate([kr_c, kr_x], axis=1)
    v_all = jnp.concatenate([v_c, v_x], axis=1)
    att_x = _mla_attend_blocked(qn_x, qr_x, kn_all, kr_all, v_all)
    y_x = _mixer_out(px, att_x, conv_w, w_out_conv, w_out_mla, w_o)
    x_new = _layer_norm(DEEPNORM_ALPHA * x + gate_x[:, None, :] * y_x, ln_g, ln_b)

    if update_ctx:
        qn_c, qr_c = _mla_q(pc[4], q_norm_g, w_uq)
        att_c = _mla_attend(qn_c, qr_c, kn_c, kr_c, v_c)
        y_c = _mixer_out(pc, att_c, conv_w, w_out_conv, w_out_mla, w_o)
        ctx = _layer_norm(DEEPNORM_ALPHA * ctx + gate_c * y_c, ln_g, ln_b)
    return x_new, ctx


def setup_inputs(seed: int = 0) -> dict:
    key = jax.random.key(seed)
    ks = jax.random.split(key, 20)
    f32 = jnp.float32

    def nrm(k, shape, s):
        return jax.random.normal(k, shape, f32) * s

    L = DEPTH
    return {
        "x": nrm(ks[0], (BATCH, SEQ, D_MODEL), 1.0),
        "c": nrm(ks[1], (BATCH, D_MODEL), 1.0),
        "ctx": nrm(ks[2], (BATCH, CTX_LEN, D_MODEL), 1.0),
        "c_ctx": nrm(ks[3], (D_MODEL,), 1.0),
        "w_ada": nrm(ks[4], (L, D_MODEL, 3 * D_MODEL), 0.5 * D_MODEL ** -0.5),
        "b_ada": nrm(ks[5], (L, 3 * D_MODEL), 0.01),
        "w_in": nrm(ks[6], (L, D_MODEL, N_IN), D_MODEL ** -0.5),
        "conv_w": nrm(ks[7], (L, CONV_K, D_CONV), CONV_K ** -0.5),
        "q_norm_g": 1.0 + nrm(ks[8], (L, Q_LORA), 0.02),
        "w_uq": nrm(ks[9], (L, Q_LORA, MLA_HEADS * (QK_NOPE + QK_ROPE)), Q_LORA ** -0.5),
        "kv_norm_g": 1.0 + nrm(ks[10], (L, KV_LORA), 0.02),
        "w_ukv": nrm(ks[11], (L, KV_LORA, MLA_HEADS * (QK_NOPE + V_DIM)), KV_LORA ** -0.5),
        "w_out_conv": nrm(ks[12], (L, D_CONV, D_MODEL), DEEPNORM_BETA * D_CONV ** -0.5),
        "w_out_mla": nrm(ks[13], (L, D_MLA, D_MODEL), DEEPNORM_BETA * D_MLA ** -0.5),
        "w_o": nrm(ks[14], (L, D_MODEL, D_MODEL), DEEPNORM_BETA * D_MODEL ** -0.5),
        "ln_g": 1.0 + nrm(ks[15], (L, D_MODEL), 0.02),
        "ln_b": nrm(ks[16], (L, D_MODEL), 0.01),
    }


def reference(x, c, ctx, c_ctx, w_ada, b_ada, w_in, conv_w, q_norm_g, w_uq, kv_norm_g, w_ukv,
              w_out_conv, w_out_mla, w_o, ln_g, ln_b):
    rope_tables = _axial_rope_tables(x.shape[1])
    c_silu = jax.nn.silu(c)
    cctx_silu = jax.nn.silu(c_ctx)
    for l in range(DEPTH):
        x, ctx = _hybrid_layer(x, ctx, c_silu, cctx_silu, rope_tables, w_ada[l], b_ada[l], w_in[l], conv_w[l],
                               q_norm_g[l], w_uq[l], kv_norm_g[l], w_ukv[l], w_out_conv[l], w_out_mla[l],
                               w_o[l], ln_g[l], ln_b[l], l < DEPTH - 1)
    return x
```

```python
import functools

import jax
import jax.numpy as jnp
import numpy as np
from jax import lax
from jax.experimental import pallas as pl
from jax.experimental.pallas import tpu as pltpu

GRID_W = 64
D_CONV = 512
MLA_HEADS = 8
QK_NOPE = 64
QK_ROPE = 32
V_DIM = 64
D_MLA = MLA_HEADS * V_DIM
Q_LORA = 256
KV_LORA = 128
ROPE_THETA = 10000.0
LN_EPS = 1e-5
RMS_EPS = 1e-6
DEPTH = 1
DEEPNORM_ALPHA = (2.0 * DEPTH) ** 0.25

LANES = 128
SUBLANES = 8
HEAD_PAD = LANES
V_ROWS = V_DIM + 16
ROW_TILE = 256
Q_TILE = 512
VMEM_LIMIT = 56 * 1024 * 1024

F32 = jnp.float32
BF16 = jnp.bfloat16


def _dot(a, b):
    return jnp.dot(a, b, preferred_element_type=F32)


def _dot_nt(a, b):
    return lax.dot_general(a, b, (((1,), (1,)), ((), ())), preferred_element_type=F32)


def _silu(t):
    return t * jax.nn.sigmoid(t)


def _rms_norm(t, g):
    return t * lax.rsqrt(jnp.mean(t * t, axis=-1, keepdims=True) + RMS_EPS) * g


def _adaln_kernel(c_ref, w_ref, b_ref, o_ref):
    cs = _silu(c_ref[...])
    o_ref[...] = jnp.dot(cs, w_ref[...], precision=lax.Precision.HIGHEST,
                         preferred_element_type=F32) + b_ref[...]


def _adaln(c_rows, w_ada, b_ada):
    rows, d = c_rows.shape
    n_out = w_ada.shape[1]
    col_tile = 512
    return pl.pallas_call(
        _adaln_kernel,
        grid=(n_out // col_tile,),
        in_specs=[
            pl.BlockSpec((rows, d), lambda i: (0, 0)),
            pl.BlockSpec((d, col_tile), lambda i: (0, i)),
            pl.BlockSpec((1, col_tile), lambda i: (0, i)),
        ],
        out_specs=pl.BlockSpec((rows, col_tile), lambda i: (0, i)),
        out_shape=jax.ShapeDtypeStruct((rows, n_out), F32),
        name="adaln",
    )(c_rows, w_ada, b_ada)


def _proj_kernel(x_ref, ctx_ref, mod_ref, tab_ref, watt_ref, gq_ref, gkv_ref, wq_ref, wk_ref, wvt_ref,
                 q_ref, k_ref, vt_ref):
    j = pl.program_id(1)
    mod = mod_ref[0]
    shift = mod[0:1]
    scale1 = 1.0 + mod[1:2]

    def keys_values(p, kr_placed):
        kvn = _rms_norm(p[:, Q_LORA:Q_LORA + KV_LORA], gkv_ref[...]).astype(BF16)
        kn = _dot(kvn, wk_ref[...])
        for h in range(MLA_HEADS):
            k_ref[0, h] = (kn[:, h * HEAD_PAD:(h + 1) * HEAD_PAD] + kr_placed).astype(BF16)
        vt = _dot_nt(wvt_ref[...], kvn)
        r = lax.broadcasted_iota(jnp.int32, vt.shape, 0)
        ones = r == V_DIM
        for h in range(1, MLA_HEADS):
            ones = ones | (r == h * V_ROWS + V_DIM)
        vt = jnp.where(ones, 1.0, vt)
        vt_ref[0] = vt.astype(BF16).reshape(MLA_HEADS, V_ROWS, vt.shape[1])

    kr0 = Q_LORA + KV_LORA

    @pl.when(j == 0)
    def _():
        h = (ctx_ref[0] * scale1 + shift).astype(BF16)
        p = _dot(h, watt_ref[...])
        keys_values(p, p[:, kr0:kr0 + LANES])

    @pl.when(j > 0)
    def _():
        h = (x_ref[0] * scale1 + shift).astype(BF16)
        p = _dot(h, watt_ref[...])
        tab = tab_ref[...]
        cq_t, sq_t = tab[:, 0:LANES], tab[:, LANES:2 * LANES]
        ck_t, sk_t = tab[:, 2 * LANES:3 * LANES], tab[:, 3 * LANES:4 * LANES]
        keys_values(p, p[:, kr0:kr0 + LANES] * ck_t + p[:, kr0 + LANES:kr0 + 2 * LANES] * sk_t)
        qn = _rms_norm(p[:, 0:Q_LORA], gq_ref[...]).astype(BF16)
        qq = _dot(qn, wq_ref[...])
        width = MLA_HEADS * HEAD_PAD
        for hd in range(MLA_HEADS):
            a = qq[:, hd * HEAD_PAD:(hd + 1) * HEAD_PAD]
            b = qq[:, width + hd * HEAD_PAD:width + (hd + 1) * HEAD_PAD]
            q_ref[0, hd] = (a * cq_t + b * sq_t).astype(BF16)


def _proj(x, ctx, mods, tab, watt, gq, gkv, wq, wk, wvt):
    bsz, n, d = x.shape
    n_ctx = ctx.shape[1]
    assert n_ctx == ROW_TILE and n % ROW_TILE == 0
    nt = n // ROW_TILE
    n_keys = n_ctx + n
    const = lambda b, j: (0, 0)
    one = pl.Buffered(1)
    return pl.pallas_call(
        _proj_kernel,
        grid=(bsz, nt + 1),
        in_specs=[
            pl.BlockSpec((1, ROW_TILE, d), lambda b, j: (b, jnp.maximum(j - 1, 0), 0)),
            pl.BlockSpec((1, ROW_TILE, d), lambda b, j: (b, 0, 0)),
            pl.BlockSpec((1, SUBLANES, d), lambda b, j: (jnp.where(j == 0, bsz, b), 0, 0)),
            pl.BlockSpec((ROW_TILE, 4 * LANES), lambda b, j: (jnp.maximum(j - 1, 0), 0)),
            pl.BlockSpec(watt.shape, const, pipeline_mode=one),
            pl.BlockSpec(gq.shape, const, pipeline_mode=one),
            pl.BlockSpec(gkv.shape, const, pipeline_mode=one),
            pl.BlockSpec(wq.shape, const, pipeline_mode=one),
            pl.BlockSpec(wk.shape, const, pipeline_mode=one),
            pl.BlockSpec(wvt.shape, const, pipeline_mode=one),
        ],
        out_specs=[
            pl.BlockSpec((1, MLA_HEADS, ROW_TILE, HEAD_PAD), lambda b, j: (b, 0, jnp.maximum(j - 1, 0), 0)),
            pl.BlockSpec((1, MLA_HEADS, ROW_TILE, HEAD_PAD), lambda b, j: (b, 0, j, 0)),
            pl.BlockSpec((1, MLA_HEADS, V_ROWS, ROW_TILE), lambda b, j: (b, 0, 0, j)),
        ],
        out_shape=[
            jax.ShapeDtypeStruct((bsz, MLA_HEADS, n, HEAD_PAD), BF16),
            jax.ShapeDtypeStruct((bsz, MLA_HEADS, n_keys, HEAD_PAD), BF16),
            jax.ShapeDtypeStruct((bsz, MLA_HEADS, V_ROWS, n_keys), BF16),
        ],
        compiler_params=pltpu.CompilerParams(
            dimension_semantics=("arbitrary", "arbitrary"), vmem_limit_bytes=VMEM_LIMIT),
        name="proj",
    )(x, ctx, mods, tab, watt, gq, gkv, wq, wk, wvt)


def _attn_kernel(q_ref, k_ref, vt_ref, o_ref):
    k = k_ref[0, 0]
    vt = vt_ref[0, 0]
    for i in range(q_ref.shape[2] // Q_TILE):
        q = q_ref[0, 0, i * Q_TILE:(i + 1) * Q_TILE, :]
        s_t = _dot_nt(k, q)
        m = jnp.max(s_t, axis=0, keepdims=True)
        p_t = jnp.exp(s_t - m).astype(BF16)
        o_t = _dot(vt, p_t)
        o_ref[0, 0, i] = o_t[0:V_DIM] / o_t[V_DIM:V_DIM + 1]


def _attn(q, k, vt):
    bsz, heads, n, _ = q.shape
    n_keys = k.shape[2]
    nq = n // Q_TILE
    return pl.pallas_call(
        _attn_kernel,
        grid=(bsz, heads),
        in_specs=[
            pl.BlockSpec((1, 1, n, HEAD_PAD), lambda b, h: (b, h, 0, 0)),
            pl.BlockSpec((1, 1, n_keys, HEAD_PAD), lambda b, h: (b, h, 0, 0)),
            pl.BlockSpec((1, 1, V_ROWS, n_keys), lambda b, h: (b, h, 0, 0)),
        ],
        out_specs=pl.BlockSpec((1, 1, nq, V_DIM, Q_TILE), lambda b, h: (b, h, 0, 0, 0)),
        out_shape=jax.ShapeDtypeStruct((bsz, heads, nq, V_DIM, Q_TILE), F32),
        compiler_params=pltpu.CompilerParams(
            dimension_semantics=("arbitrary", "arbitrary"), vmem_limit_bytes=VMEM_LIMIT),
        name="attn",
    )(q, k, vt)


def _mixer_kernel(x_ref, xp_ref, xn_ref, mod_ref, att_ref, wc_ref, cw_ref, woc_ref, wom_ref, wo_ref, ln_ref,
                  o_ref):
    j = pl.program_id(1)
    last = pl.num_programs(1) - 1
    d = x_ref.shape[2]
    x = x_ref[0]
    mod = mod_ref[0]
    shift = mod[0:1]
    scale1 = 1.0 + mod[1:2]
    gate = mod[2:3]
    hx = (x * scale1 + shift).astype(BF16)

    p1 = _dot(hx, wc_ref[:, 0:2 * D_CONV])
    u = p1[:, :D_CONV] * p1[:, D_CONV:]
    xh = jnp.concatenate([xp_ref[0], xn_ref[0]], axis=0)
    ph = _dot((xh * scale1 + shift).astype(BF16), wc_ref[:, 0:2 * D_CONV])
    uh = ph[:, :D_CONV] * ph[:, D_CONV:]
    prev_edge = uh[SUBLANES - 1:SUBLANES] * (j > 0).astype(F32)
    next_edge = uh[SUBLANES:SUBLANES + 1] * (j < last).astype(F32)
    rows = u.shape[0]
    r = lax.broadcasted_iota(jnp.int32, u.shape, 0)
    u_prev = jnp.where(r == 0, prev_edge, pltpu.roll(u, 1, 0))
    u_next = jnp.where(r == rows - 1, next_edge, pltpu.roll(u, rows - 1, 0))
    cw = cw_ref[...]
    conv = u_prev * cw[0:1] + u * cw[1:2] + u_next * cw[2:3]

    p2 = _dot(hx, wc_ref[:, 2 * D_CONV:4 * D_CONV])
    zc = _silu(p2[:, D_CONV:]) * p2[:, :D_CONV] * conv
    yc = _dot(zc.astype(BF16), woc_ref[...])

    o0 = 4 * D_CONV
    gm = _dot(hx, wc_ref[:, o0:o0 + D_MLA])
    att = att_ref[0, :, 0].reshape(D_MLA, rows).T
    ym = _dot((_silu(gm) * att).astype(BF16), wom_ref[...])

    o1 = o0 + D_MLA
    g_conv = _dot(hx, wc_ref[:, o1:o1 + d])
    g_mla = _dot(hx, wc_ref[:, o1 + d:o1 + 2 * d])
    merged = jax.nn.sigmoid(g_conv) * yc + jax.nn.sigmoid(g_mla) * ym
    y = _dot(merged.astype(BF16), wo_ref[...])

    res = DEEPNORM_ALPHA * x + gate * y
    mu = jnp.mean(res, axis=-1, keepdims=True)
    dev = res - mu
    var = jnp.mean(dev * dev, axis=-1, keepdims=True)
    ln = ln_ref[...]
    o_ref[0] = dev * lax.rsqrt(var + LN_EPS) * ln[0:1] + ln[1:2]


def _mixer(x, mods, att_t, wc, cw, woc, wom, wo, ln):
    bsz, n, d = x.shape
    nt = n // ROW_TILE
    per_q = Q_TILE // ROW_TILE
    blocks8 = ROW_TILE // SUBLANES
    const = lambda b, j: (0, 0)
    one = pl.Buffered(1)
    return pl.pallas_call(
        _mixer_kernel,
        grid=(bsz, nt),
        in_specs=[
            pl.BlockSpec((1, ROW_TILE, d), lambda b, j: (b, j, 0)),
            pl.BlockSpec((1, SUBLANES, d), lambda b, j: (b, jnp.maximum(j * blocks8 - 1, 0), 0)),
            pl.BlockSpec((1, SUBLANES, d), lambda b, j: (b, jnp.minimum((j + 1) * blocks8, n // SUBLANES - 1), 0)),
            pl.BlockSpec((1, SUBLANES, d), lambda b, j: (b, 0, 0)),
            pl.BlockSpec((1, MLA_HEADS, 1, V_DIM, ROW_TILE), lambda b, j: (b, 0, j // per_q, 0, j % per_q)),
            pl.BlockSpec(wc.shape, const, pipeline_mode=one),
            pl.BlockSpec(cw.shape, const, pipeline_mode=one),
            pl.BlockSpec(woc.shape, const, pipeline_mode=one),
            pl.BlockSpec(wom.shape, const, pipeline_mode=one),
            pl.BlockSpec(wo.shape, const, pipeline_mode=one),
            pl.BlockSpec(ln.shape, const, pipeline_mode=one),
        ],
        out_specs=pl.BlockSpec((1, ROW_TILE, d), lambda b, j: (b, j, 0)),
        out_shape=jax.ShapeDtypeStruct((bsz, n, d), F32),
        compiler_params=pltpu.CompilerParams(
            dimension_semantics=("arbitrary", "arbitrary"), vmem_limit_bytes=VMEM_LIMIT),
        name="mixer",
    )(x, x, x, mods, att_t, wc, cw, woc, wom, wo, ln)


def _rotate_cols(w):
    q = QK_ROPE // 4
    idx = np.concatenate([np.arange(q, 2 * q), np.arange(0, q), np.arange(3 * q, 4 * q), np.arange(2 * q, 3 * q)])
    sign = np.concatenate([-np.ones(q), np.ones(q), -np.ones(q), np.ones(q)]).astype(np.float32)
    return w[..., idx] * sign


def _rope_tables(n, scale):
    rows = n // GRID_W
    row_pos = jnp.broadcast_to(jnp.arange(rows)[:, None], (rows, GRID_W)).reshape(-1).astype(F32)
    col_pos = jnp.broadcast_to(jnp.arange(GRID_W)[None, :], (rows, GRID_W)).reshape(-1).astype(F32)
    axis_dim = QK_ROPE // 2
    inv_freq = ROPE_THETA ** (-jnp.arange(0, axis_dim, 2, dtype=F32) / axis_dim)
    ang_r = row_pos[:, None] * inv_freq[None, :]
    ang_c = col_pos[:, None] * inv_freq[None, :]
    cos = jnp.concatenate([jnp.cos(ang_r), jnp.cos(ang_r), jnp.cos(ang_c), jnp.cos(ang_c)], axis=-1)
    sin = jnp.concatenate([jnp.sin(ang_r), jnp.sin(ang_r), jnp.sin(ang_c), jnp.sin(ang_c)], axis=-1)
    pad = jnp.zeros((n, HEAD_PAD - QK_NOPE - QK_ROPE), F32)
    zero_n = jnp.zeros((n, QK_NOPE), F32)
    cq = jnp.concatenate([jnp.full((n, QK_NOPE), scale, F32), cos * scale, pad], axis=-1)
    sq = jnp.concatenate([zero_n, sin * scale, pad], axis=-1)
    ck = jnp.concatenate([zero_n, cos, pad], axis=-1)
    sk = jnp.concatenate([zero_n, sin, pad], axis=-1)
    return jnp.concatenate([cq, sq, ck, sk], axis=-1)


def _place_rope(w):
    d = w.shape[0]
    return jnp.concatenate([jnp.zeros((d, QK_NOPE), w.dtype), w,
                            jnp.zeros((d, HEAD_PAD - QK_NOPE - QK_ROPE), w.dtype)], axis=-1)


def kernel(x, c, ctx, c_ctx, w_ada, b_ada, w_in, conv_w, q_norm_g, w_uq, kv_norm_g, w_ukv, w_out_conv,
           w_out_mla, w_o, ln_g, ln_b):
    bsz, n, d = x.shape
    assert w_ada.shape[0] == DEPTH == 1
    w_ada, b_ada, w_in, conv_w = w_ada[0], b_ada[0], w_in[0], conv_w[0]
    q_norm_g, w_uq, kv_norm_g, w_ukv = q_norm_g[0], w_uq[0], kv_norm_g[0], w_ukv[0]
    w_out_conv, w_out_mla, w_o, ln_g, ln_b = w_out_conv[0], w_out_mla[0], w_o[0], ln_g[0], ln_b[0]

    pad_rows = (-(bsz + 1)) % SUBLANES
    c_rows = jnp.concatenate([c, c_ctx[None, :], jnp.zeros((pad_rows, d), F32)], axis=0)
    mod = _adaln(c_rows, w_ada, b_ada[None, :])
    mods = mod[:bsz + 1].reshape(bsz + 1, 3, d)
    mods = jnp.concatenate([mods, jnp.zeros((bsz + 1, SUBLANES - 3, d), F32)], axis=1)

    o_xc, o_bc, o_cc, o_gc = 0, D_CONV, 2 * D_CONV, 3 * D_CONV
    o_cq = 4 * D_CONV
    o_ckv = o_cq + Q_LORA
    o_kr = o_ckv + KV_LORA
    o_gm = o_kr + QK_ROPE
    o_gconv = o_gm + D_MLA
    o_gmla = o_gconv + d
    col = lambda o, w: w_in[:, o:o + w]

    w_kr = col(o_kr, QK_ROPE)
    watt = jnp.concatenate([col(o_cq, Q_LORA), col(o_ckv, KV_LORA), _place_rope(w_kr),
                            _place_rope(_rotate_cols(w_kr))], axis=-1).astype(BF16)

    hq = QK_NOPE + QK_ROPE
    w_uq_h = w_uq.reshape(Q_LORA, MLA_HEADS, hq)
    zq = jnp.zeros((Q_LORA, MLA_HEADS, HEAD_PAD - hq), F32)
    wq_plain = jnp.concatenate([w_uq_h, zq], axis=-1).reshape(Q_LORA, MLA_HEADS * HEAD_PAD)
    wq_rot = jnp.concatenate([jnp.zeros((Q_LORA, MLA_HEADS, QK_NOPE), F32),
                              _rotate_cols(w_uq_h[..., QK_NOPE:]), zq], axis=-1).reshape(Q_LORA, MLA_HEADS * HEAD_PAD)
    wq = jnp.concatenate([wq_plain, wq_rot], axis=-1).astype(BF16)

    w_ukv_h = w_ukv.reshape(KV_LORA, MLA_HEADS, QK_NOPE + V_DIM)
    wk = jnp.concatenate([w_ukv_h[..., :QK_NOPE], jnp.zeros((KV_LORA, MLA_HEADS, HEAD_PAD - QK_NOPE), F32)],
                         axis=-1).reshape(KV_LORA, MLA_HEADS * HEAD_PAD).astype(BF16)
    wv = jnp.concatenate([w_ukv_h[..., QK_NOPE:], jnp.zeros((KV_LORA, MLA_HEADS, V_ROWS - V_DIM), F32)], axis=-1)
    wvt = wv.reshape(KV_LORA, MLA_HEADS * V_ROWS).T.astype(BF16)

    tab = _rope_tables(n, float((QK_NOPE + QK_ROPE) ** -0.5))
    q, k, vt = _proj(x, ctx, mods, tab, watt, q_norm_g[None, :], kv_norm_g[None, :], wq, wk, wvt)
    att_t = _attn(q, k, vt)

    wc = jnp.concatenate([col(o_xc, D_CONV), col(o_cc, D_CONV), col(o_bc, D_CONV), col(o_gc, D_CONV),
                          col(o_gm, D_MLA), col(o_gconv, d), col(o_gmla, d)], axis=-1).astype(BF16)
    cw = jnp.concatenate([conv_w, jnp.zeros((SUBLANES - conv_w.shape[0], D_CONV), F32)], axis=0)
    ln = jnp.concatenate([ln_g[None, :], ln_b[None, :], jnp.zeros((SUBLANES - 2, d), F32)], axis=0)
    return _mixer(x, mods, att_t, wc, cw, w_out_conv.astype(BF16), w_out_mla.astype(BF16), w_o.astype(BF16), ln)
```

```python
import functools

import jax
import jax.numpy as jnp
import numpy as np
from jax import lax
from jax.experimental import pallas as pl
from jax.experimental.pallas import tpu as pltpu

GRID_W = 64
D_CONV = 512
MLA_HEADS = 8
QK_NOPE = 64
QK_ROPE = 32
V_DIM = 64
D_MLA = MLA_HEADS * V_DIM
Q_LORA = 256
KV_LORA = 128
ROPE_THETA = 10000.0
LN_EPS = 1e-5
RMS_EPS = 1e-6
DEPTH = 1
DEEPNORM_ALPHA = (2.0 * DEPTH) ** 0.25

LANES = 128
SUBLANES = 8
HEAD_PAD = LANES
V_ROWS = V_DIM + 16
ROW_TILE = 256
Q_TILE = 512
VMEM_LIMIT = 56 * 1024 * 1024

F32 = jnp.float32
BF16 = jnp.bfloat16


def _dot(a, b):
    return jnp.dot(a, b, preferred_element_type=F32)


def _dot_nt(a, b):
    return lax.dot_general(a, b, (((1,), (1,)), ((), ())), preferred_element_type=F32)


def _silu(t):
    return t * jax.nn.sigmoid(t)


def _rms_norm(t, g):
    return t * lax.rsqrt(jnp.mean(t * t, axis=-1, keepdims=True) + RMS_EPS) * g


def _adaln_kernel(c_ref, w_ref, b_ref, o_ref):
    cs = _silu(c_ref[...])
    o_ref[...] = jnp.dot(cs, w_ref[...], precision=lax.Precision.HIGHEST,
                         preferred_element_type=F32) + b_ref[...]


def _adaln(c_rows, w_ada, b_ada):
    rows, d = c_rows.shape
    n_out = w_ada.shape[1]
    col_tile = 512
    return pl.pallas_call(
        _adaln_kernel,
        grid=(n_out // col_tile,),
        in_specs=[
            pl.BlockSpec((rows, d), lambda i: (0, 0)),
            pl.BlockSpec((d, col_tile), lambda i: (0, i)),
            pl.BlockSpec((1, col_tile), lambda i: (0, i)),
        ],
        out_specs=pl.BlockSpec((rows, col_tile), lambda i: (0, i)),
        out_shape=jax.ShapeDtypeStruct((rows, n_out), F32),
        name="adaln",
    )(c_rows, w_ada, b_ada)


def _proj_kernel(x_ref, ctx_ref, mod_ref, tab_ref, watt_ref, gq_ref, gkv_ref, wq_ref, wk_ref, wvt_ref,
                 q_ref, k_ref, vt_ref):
    j = pl.program_id(1)
    mod = mod_ref[0]
    shift = mod[0:1]
    scale1 = 1.0 + mod[1:2]

    def keys_values(p, kr_placed):
        kvn = _rms_norm(p[:, Q_LORA:Q_LORA + KV_LORA], gkv_ref[...]).astype(BF16)
        kn = _dot(kvn, wk_ref[...])
        for h in range(MLA_HEADS):
            k_ref[0, h] = (kn[:, h * HEAD_PAD:(h + 1) * HEAD_PAD] + kr_placed).astype(BF16)
        vt = _dot_nt(wvt_ref[...], kvn)
        r = lax.broadcasted_iota(jnp.int32, vt.shape, 0)
        ones = r == V_DIM
        for h in range(1, MLA_HEADS):
            ones = ones | (r == h * V_ROWS + V_DIM)
        vt = jnp.where(ones, 1.0, vt)
        vt_ref[0] = vt.astype(BF16).reshape(MLA_HEADS, V_ROWS, vt.shape[1])

    kr0 = Q_LORA + KV_LORA

    @pl.when(j == 0)
    def _():
        h = (ctx_ref[0] * scale1 + shift).astype(BF16)
        p = _dot(h, watt_ref[...])
        keys_values(p, p[:, kr0:kr0 + LANES])

    @pl.when(j > 0)
    def _():
        h = (x_ref[0] * scale1 + shift).astype(BF16)
        p = _dot(h, watt_ref[...])
        tab = tab_ref[...]
        cq_t, sq_t = tab[:, 0:LANES], tab[:, LANES:2 * LANES]
        ck_t, sk_t = tab[:, 2 * LANES:3 * LANES], tab[:, 3 * LANES:4 * LANES]
        keys_values(p, p[:, kr0:kr0 + LANES] * ck_t + p[:, kr0 + LANES:kr0 + 2 * LANES] * sk_t)
        qn = _rms_norm(p[:, 0:Q_LORA], gq_ref[...]).astype(BF16)
        qq = _dot(qn, wq_ref[...])
        width = MLA_HEADS * HEAD_PAD
        for hd in range(MLA_HEADS):
            a = qq[:, hd * HEAD_PAD:(hd + 1) * HEAD_PAD]
            b = qq[:, width + hd * HEAD_PAD:width + (hd + 1) * HEAD_PAD]
            q_ref[0, hd] = (a * cq_t + b * sq_t).astype(BF16)


def _proj(x, ctx, mods, tab, watt, gq, gkv, wq, wk, wvt):
    bsz, n, d = x.shape
    n_ctx = ctx.shape[1]
    assert n_ctx == ROW_TILE and n % ROW_TILE == 0
    nt = n // ROW_TILE
    n_keys = n_ctx + n
    const = lambda b, j: (0, 0)
    one = pl.Buffered(1)
    return pl.pallas_call(
        _proj_kernel,
        grid=(bsz, nt + 1),
        in_specs=[
            pl.BlockSpec((1, ROW_TILE, d), lambda b, j: (b, jnp.maximum(j - 1, 0), 0)),
            pl.BlockSpec((1, ROW_TILE, d), lambda b, j: (b, 0, 0)),
            pl.BlockSpec((1, SUBLANES, d), lambda b, j: (jnp.where(j == 0, bsz, b), 0, 0)),
            pl.BlockSpec((ROW_TILE, 4 * LANES), lambda b, j: (jnp.maximum(j - 1, 0), 0)),
            pl.BlockSpec(watt.shape, const, pipeline_mode=one),
            pl.BlockSpec(gq.shape, const, pipeline_mode=one),
            pl.BlockSpec(gkv.shape, const, pipeline_mode=one),
            pl.BlockSpec(wq.shape, const, pipeline_mode=one),
            pl.BlockSpec(wk.shape, const, pipeline_mode=one),
            pl.BlockSpec(wvt.shape, const, pipeline_mode=one),
        ],
        out_specs=[
            pl.BlockSpec((1, MLA_HEADS, ROW_TILE, HEAD_PAD), lambda b, j: (b, 0, jnp.maximum(j - 1, 0), 0)),
            pl.BlockSpec((1, MLA_HEADS, ROW_TILE, HEAD_PAD), lambda b, j: (b, 0, j, 0)),
            pl.BlockSpec((1, MLA_HEADS, V_ROWS, ROW_TILE), lambda b, j: (b, 0, 0, j)),
        ],
        out_shape=[
            jax.ShapeDtypeStruct((bsz, MLA_HEADS, n, HEAD_PAD), BF16),
            jax.ShapeDtypeStruct((bsz, MLA_HEADS, n_keys, HEAD_PAD), BF16),
            jax.ShapeDtypeStruct((bsz, MLA_HEADS, V_ROWS, n_keys), BF16),
        ],
        compiler_params=pltpu.CompilerParams(
            dimension_semantics=("arbitrary", "arbitrary"), vmem_limit_bytes=VMEM_LIMIT),
        name="proj",
    )(x, ctx, mods, tab, watt, gq, gkv, wq, wk, wvt)


def _attn_kernel(q_ref, k_ref, vt_ref, o_ref, s_ref):
    heads = q_ref.shape[1]
    nq = q_ref.shape[2] // Q_TILE

    def scores(h, i, slot):
        q = q_ref[0, h, i * Q_TILE:(i + 1) * Q_TILE, :]
        s_ref[slot] = _dot_nt(k_ref[0, h], q)

    def finish(h, i, slot):
        s_t = s_ref[slot]
        m = jnp.max(s_t, axis=0, keepdims=True)
        p_t = jnp.exp2(s_t - m).astype(BF16)
        o_t = _dot(vt_ref[0, h], p_t)
        o_ref[0, h, i] = o_t[0:V_DIM] * (1.0 / o_t[V_DIM:V_DIM + 1])

    def head(h, next_h):
        for i in range(nq):
            slot = i % 2
            if i + 1 < nq:
                scores(h, i + 1, 1 - slot)
            elif next_h is not None:
                scores(next_h, 0, 1 - slot)
            finish(h, i, slot)

    assert nq % 2 == 0
    scores(0, 0, 0)

    def body(h, carry):
        head(h, h + 1)
        return carry

    lax.fori_loop(0, heads - 1, body, 0)
    head(heads - 1, None)


def _attn(q, k, vt):
    bsz, heads, n, _ = q.shape
    n_keys = k.shape[2]
    nq = n // Q_TILE
    return pl.pallas_call(
        _attn_kernel,
        grid=(bsz,),
        in_specs=[
            pl.BlockSpec((1, heads, n, HEAD_PAD), lambda b: (b, 0, 0, 0)),
            pl.BlockSpec((1, heads, n_keys, HEAD_PAD), lambda b: (b, 0, 0, 0)),
            pl.BlockSpec((1, heads, V_ROWS, n_keys), lambda b: (b, 0, 0, 0)),
        ],
        out_specs=pl.BlockSpec((1, heads, nq, V_DIM, Q_TILE), lambda b: (b, 0, 0, 0, 0)),
        out_shape=jax.ShapeDtypeStruct((bsz, heads, nq, V_DIM, Q_TILE), F32),
        scratch_shapes=[pltpu.VMEM((2, n_keys, Q_TILE), F32)],
        compiler_params=pltpu.CompilerParams(
            dimension_semantics=("arbitrary",), vmem_limit_bytes=VMEM_LIMIT),
        name="attn",
    )(q, k, vt)


def _mixer_kernel(x_ref, xp_ref, xn_ref, mod_ref, att_ref, wc_ref, cw_ref, woc_ref, wom_ref, wo_ref, ln_ref,
                  o_ref):
    j = pl.program_id(1)
    last = pl.num_programs(1) - 1
    d = x_ref.shape[2]
    x = x_ref[0]
    mod = mod_ref[0]
    shift = mod[0:1]
    scale1 = 1.0 + mod[1:2]
    gate = mod[2:3]
    hx = (x * scale1 + shift).astype(BF16)

    p1 = _dot(hx, wc_ref[:, 0:2 * D_CONV])
    u = p1[:, :D_CONV] * p1[:, D_CONV:]
    xh = jnp.concatenate([xp_ref[0], xn_ref[0]], axis=0)
    ph = _dot((xh * scale1 + shift).astype(BF16), wc_ref[:, 0:2 * D_CONV])
    uh = ph[:, :D_CONV] * ph[:, D_CONV:]
    prev_edge = uh[SUBLANES - 1:SUBLANES] * (j > 0).astype(F32)
    next_edge = uh[SUBLANES:SUBLANES + 1] * (j < last).astype(F32)
    rows = u.shape[0]
    r = lax.broadcasted_iota(jnp.int32, u.shape, 0)
    u_prev = jnp.where(r == 0, prev_edge, pltpu.roll(u, 1, 0))
    u_next = jnp.where(r == rows - 1, next_edge, pltpu.roll(u, rows - 1, 0))
    cw = cw_ref[...]
    conv = u_prev * cw[0:1] + u * cw[1:2] + u_next * cw[2:3]

    p2 = _dot(hx, wc_ref[:, 2 * D_CONV:4 * D_CONV])
    zc = _silu(p2[:, D_CONV:]) * p2[:, :D_CONV] * conv
    yc = _dot(zc.astype(BF16), woc_ref[...])

    o0 = 4 * D_CONV
    gm = _dot(hx, wc_ref[:, o0:o0 + D_MLA])
    att = att_ref[0, :, 0].reshape(D_MLA, rows).T
    ym = _dot((_silu(gm) * att).astype(BF16), wom_ref[...])

    o1 = o0 + D_MLA
    g_conv = _dot(hx, wc_ref[:, o1:o1 + d])
    g_mla = _dot(hx, wc_ref[:, o1 + d:o1 + 2 * d])
    merged = jax.nn.sigmoid(g_conv) * yc + jax.nn.sigmoid(g_mla) * ym
    y = _dot(merged.astype(BF16), wo_ref[...])

    res = DEEPNORM_ALPHA * x + gate * y
    mu = jnp.mean(res, axis=-1, keepdims=True)
    dev = res - mu
    var = jnp.mean(dev * dev, axis=-1, keepdims=True)
    ln = ln_ref[...]
    o_ref[0] = dev * lax.rsqrt(var + LN_EPS) * ln[0:1] + ln[1:2]


def _mixer(x, mods, att_t, wc, cw, woc, wom, wo, ln):
    bsz, n, d = x.shape
    nt = n // ROW_TILE
    per_q = Q_TILE // ROW_TILE
    blocks8 = ROW_TILE // SUBLANES
    const = lambda b, j: (0, 0)
    one = pl.Buffered(1)
    return pl.pallas_call(
        _mixer_kernel,
        grid=(bsz, nt),
        in_specs=[
            pl.BlockSpec((1, ROW_TILE, d), lambda b, j: (b, j, 0)),
            pl.BlockSpec((1, SUBLANES, d), lambda b, j: (b, jnp.maximum(j * blocks8 - 1, 0), 0)),
            pl.BlockSpec((1, SUBLANES, d), lambda b, j: (b, jnp.minimum((j + 1) * blocks8, n // SUBLANES - 1), 0)),
            pl.BlockSpec((1, SUBLANES, d), lambda b, j: (b, 0, 0)),
            pl.BlockSpec((1, MLA_HEADS, 1, V_DIM, ROW_TILE), lambda b, j: (b, 0, j // per_q, 0, j % per_q)),
            pl.BlockSpec(wc.shape, const, pipeline_mode=one),
            pl.BlockSpec(cw.shape, const, pipeline_mode=one),
            pl.BlockSpec(woc.shape, const, pipeline_mode=one),
            pl.BlockSpec(wom.shape, const, pipeline_mode=one),
            pl.BlockSpec(wo.shape, const, pipeline_mode=one),
            pl.BlockSpec(ln.shape, const, pipeline_mode=one),
        ],
        out_specs=pl.BlockSpec((1, ROW_TILE, d), lambda b, j: (b, j, 0)),
        out_shape=jax.ShapeDtypeStruct((bsz, n, d), F32),
        compiler_params=pltpu.CompilerParams(
            dimension_semantics=("arbitrary", "arbitrary"), vmem_limit_bytes=VMEM_LIMIT),
        name="mixer",
    )(x, x, x, mods, att_t, wc, cw, woc, wom, wo, ln)


def _rotate_cols(w):
    q = QK_ROPE // 4
    idx = np.concatenate([np.arange(q, 2 * q), np.arange(0, q), np.arange(3 * q, 4 * q), np.arange(2 * q, 3 * q)])
    sign = np.concatenate([-np.ones(q), np.ones(q), -np.ones(q), np.ones(q)]).astype(np.float32)
    return w[..., idx] * sign


def _rope_tables(n, scale):
    rows = n // GRID_W
    row_pos = jnp.broadcast_to(jnp.arange(rows)[:, None], (rows, GRID_W)).reshape(-1).astype(F32)
    col_pos = jnp.broadcast_to(jnp.arange(GRID_W)[None, :], (rows, GRID_W)).reshape(-1).astype(F32)
    axis_dim = QK_ROPE // 2
    inv_freq = ROPE_THETA ** (-jnp.arange(0, axis_dim, 2, dtype=F32) / axis_dim)
    ang_r = row_pos[:, None] * inv_freq[None, :]
    ang_c = col_pos[:, None] * inv_freq[None, :]
    cos = jnp.concatenate([jnp.cos(ang_r), jnp.cos(ang_r), jnp.cos(ang_c), jnp.cos(ang_c)], axis=-1)
    sin = jnp.concatenate([jnp.sin(ang_r), jnp.sin(ang_r), jnp.sin(ang_c), jnp.sin(ang_c)], axis=-1)
    pad = jnp.zeros((n, HEAD_PAD - QK_NOPE - QK_ROPE), F32)
    zero_n = jnp.zeros((n, QK_NOPE), F32)
    cq = jnp.concatenate([jnp.full((n, QK_NOPE), scale, F32), cos * scale, pad], axis=-1)
    sq = jnp.concatenate([zero_n, sin * scale, pad], axis=-1)
    ck = jnp.concatenate([zero_n, cos, pad], axis=-1)
    sk = jnp.concatenate([zero_n, sin, pad], axis=-1)
    return jnp.concatenate([cq, sq, ck, sk], axis=-1)


def _place_rope(w):
    d = w.shape[0]
    return jnp.concatenate([jnp.zeros((d, QK_NOPE), w.dtype), w,
                            jnp.zeros((d, HEAD_PAD - QK_NOPE - QK_ROPE), w.dtype)], axis=-1)


def kernel(x, c, ctx, c_ctx, w_ada, b_ada, w_in, conv_w, q_norm_g, w_uq, kv_norm_g, w_ukv, w_out_conv,
           w_out_mla, w_o, ln_g, ln_b):
    bsz, n, d = x.shape
    assert w_ada.shape[0] == DEPTH == 1
    w_ada, b_ada, w_in, conv_w = w_ada[0], b_ada[0], w_in[0], conv_w[0]
    q_norm_g, w_uq, kv_norm_g, w_ukv = q_norm_g[0], w_uq[0], kv_norm_g[0], w_ukv[0]
    w_out_conv, w_out_mla, w_o, ln_g, ln_b = w_out_conv[0], w_out_mla[0], w_o[0], ln_g[0], ln_b[0]

    pad_rows = (-(bsz + 1)) % SUBLANES
    c_rows = jnp.concatenate([c, c_ctx[None, :], jnp.zeros((pad_rows, d), F32)], axis=0)
    mod = _adaln(c_rows, w_ada, b_ada[None, :])
    mods = mod[:bsz + 1].reshape(bsz + 1, 3, d)
    mods = jnp.concatenate([mods, jnp.zeros((bsz + 1, SUBLANES - 3, d), F32)], axis=1)

    o_xc, o_bc, o_cc, o_gc = 0, D_CONV, 2 * D_CONV, 3 * D_CONV
    o_cq = 4 * D_CONV
    o_ckv = o_cq + Q_LORA
    o_kr = o_ckv + KV_LORA
    o_gm = o_kr + QK_ROPE
    o_gconv = o_gm + D_MLA
    o_gmla = o_gconv + d
    col = lambda o, w: w_in[:, o:o + w]

    w_kr = col(o_kr, QK_ROPE)
    watt = jnp.concatenate([col(o_cq, Q_LORA), col(o_ckv, KV_LORA), _place_rope(w_kr),
                            _place_rope(_rotate_cols(w_kr))], axis=-1).astype(BF16)

    hq = QK_NOPE + QK_ROPE
    w_uq_h = w_uq.reshape(Q_LORA, MLA_HEADS, hq)
    zq = jnp.zeros((Q_LORA, MLA_HEADS, HEAD_PAD - hq), F32)
    wq_plain = jnp.concatenate([w_uq_h, zq], axis=-1).reshape(Q_LORA, MLA_HEADS * HEAD_PAD)
    wq_rot = jnp.concatenate([jnp.zeros((Q_LORA, MLA_HEADS, QK_NOPE), F32),
                              _rotate_cols(w_uq_h[..., QK_NOPE:]), zq], axis=-1).reshape(Q_LORA, MLA_HEADS * HEAD_PAD)
    wq = jnp.concatenate([wq_plain, wq_rot], axis=-1).astype(BF16)

    w_ukv_h = w_ukv.reshape(KV_LORA, MLA_HEADS, QK_NOPE + V_DIM)
    wk = jnp.concatenate([w_ukv_h[..., :QK_NOPE], jnp.zeros((KV_LORA, MLA_HEADS, HEAD_PAD - QK_NOPE), F32)],
                         axis=-1).reshape(KV_LORA, MLA_HEADS * HEAD_PAD).astype(BF16)
    wv = jnp.concatenate([w_ukv_h[..., QK_NOPE:], jnp.zeros((KV_LORA, MLA_HEADS, V_ROWS - V_DIM), F32)], axis=-1)
    wvt = wv.reshape(KV_LORA, MLA_HEADS * V_ROWS).T.astype(BF16)

    tab = _rope_tables(n, float((QK_NOPE + QK_ROPE) ** -0.5 * np.log2(np.e)))
    q, k, vt = _proj(x, ctx, mods, tab, watt, q_norm_g[None, :], kv_norm_g[None, :], wq, wk, wvt)
    att_t = _attn(q, k, vt)

    wc = jnp.concatenate([col(o_xc, D_CONV), col(o_cc, D_CONV), col(o_bc, D_CONV), col(o_gc, D_CONV),
                          col(o_gm, D_MLA), col(o_gconv, d), col(o_gmla, d)], axis=-1).astype(BF16)
    cw = jnp.concatenate([conv_w, jnp.zeros((SUBLANES - conv_w.shape[0], D_CONV), F32)], axis=0)
    ln = jnp.concatenate([ln_g[None, :], ln_b[None, :], jnp.zeros((SUBLANES - 2, d), F32)], axis=0)
    return _mixer(x, mods, att_t, wc, cw, w_out_conv.astype(BF16), w_out_mla.astype(BF16), w_o.astype(BF16), ln)
```

```python
import jax
import jax.numpy as jnp
import numpy as np
from jax import lax
from jax.experimental import pallas as pl
from jax.experimental.pallas import tpu as pltpu

GRID_W = 64
D_CONV = 512
MLA_HEADS = 8
QK_NOPE = 64
QK_ROPE = 32
V_DIM = 64
D_MLA = MLA_HEADS * V_DIM
Q_LORA = 256
KV_LORA = 128
ROPE_THETA = 10000.0
LN_EPS = 1e-5
RMS_EPS = 1e-6
DEPTH = 1
DEEPNORM_ALPHA = (2.0 * DEPTH) ** 0.25

O_XC, O_BC, O_CC, O_GC = 0, D_CONV, 2 * D_CONV, 3 * D_CONV
O_CQ = 4 * D_CONV
O_CKV = O_CQ + Q_LORA
O_KR = O_CKV + KV_LORA
O_GM = O_KR + QK_ROPE
O_GCONV = O_GM + D_MLA

LANES = 128
SUBLANES = 8
HEAD_PAD = LANES
V_ROWS = V_DIM + 16
ROW_TILE = 256
MIX_TILE = 512
Q_TILE = 512
VMEM_LIMIT = 56 * 1024 * 1024

F32 = jnp.float32
BF16 = jnp.bfloat16


def _dot(a, b):
    return jnp.dot(a, b, preferred_element_type=F32)


def _dot_nt(a, b):
    return lax.dot_general(a, b, (((1,), (1,)), ((), ())), preferred_element_type=F32)


def _silu(t):
    return t * jax.nn.sigmoid(t)


def _rms_norm(t, g):
    return t * lax.rsqrt(jnp.mean(t * t, axis=-1, keepdims=True) + RMS_EPS) * g


def _adaln_kernel(c_ref, w_ref, b_ref, o_ref):
    cs = _silu(c_ref[...])
    o_ref[...] = jnp.dot(cs, w_ref[...], precision=lax.Precision.HIGHEST,
                         preferred_element_type=F32) + b_ref[...]


def _adaln(c_rows, w_ada, b_ada):
    rows, d = c_rows.shape
    n_out = w_ada.shape[1]
    col_tile = 512
    return pl.pallas_call(
        _adaln_kernel,
        grid=(n_out // col_tile,),
        in_specs=[
            pl.BlockSpec((rows, d), lambda i: (0, 0)),
            pl.BlockSpec((d, col_tile), lambda i: (0, i)),
            pl.BlockSpec((1, col_tile), lambda i: (0, i)),
        ],
        out_specs=pl.BlockSpec((rows, col_tile), lambda i: (0, i)),
        out_shape=jax.ShapeDtypeStruct((rows, n_out), F32),
        name="adaln",
    )(c_rows, w_ada, b_ada)


def _proj_kernel(x_ref, ctx_ref, mod_ref, tab_ref, watt_ref, gq_ref, gkv_ref, wq_ref, wk_ref, wvt_ref,
                 q_ref, k_ref, vt_ref):
    j = pl.program_id(1)
    mod = mod_ref[0]
    shift = mod[0:1]
    scale1 = 1.0 + mod[1:2]

    def keys_values(p, kr_placed):
        kvn = _rms_norm(p[:, Q_LORA:Q_LORA + KV_LORA], gkv_ref[...]).astype(BF16)
        kn = _dot(kvn, wk_ref[...])
        for h in range(MLA_HEADS):
            k_ref[0, h] = (kn[:, h * HEAD_PAD:(h + 1) * HEAD_PAD] + kr_placed).astype(BF16)
        vt = _dot_nt(wvt_ref[...], kvn)
        r = lax.broadcasted_iota(jnp.int32, vt.shape, 0)
        ones = r == V_DIM
        for h in range(1, MLA_HEADS):
            ones = ones | (r == h * V_ROWS + V_DIM)
        vt = jnp.where(ones, 1.0, vt)
        vt_ref[0] = vt.astype(BF16).reshape(MLA_HEADS, V_ROWS, vt.shape[1])

    kr0 = Q_LORA + KV_LORA

    @pl.when(j == 0)
    def _():
        h = (ctx_ref[0] * scale1 + shift).astype(BF16)
        p = _dot_nt(h, watt_ref[...])
        keys_values(p, p[:, kr0:kr0 + LANES])

    @pl.when(j > 0)
    def _():
        h = (x_ref[0] * scale1 + shift).astype(BF16)
        p = _dot_nt(h, watt_ref[...])
        tab = tab_ref[...]
        cq_t, sq_t = tab[:, 0:LANES], tab[:, LANES:2 * LANES]
        ck_t, sk_t = tab[:, 2 * LANES:3 * LANES], tab[:, 3 * LANES:4 * LANES]
        keys_values(p, p[:, kr0:kr0 + LANES] * ck_t + p[:, kr0 + LANES:kr0 + 2 * LANES] * sk_t)
        qn = _rms_norm(p[:, 0:Q_LORA], gq_ref[...]).astype(BF16)
        qq = _dot(qn, wq_ref[...])
        width = MLA_HEADS * HEAD_PAD
        for hd in range(MLA_HEADS):
            a = qq[:, hd * HEAD_PAD:(hd + 1) * HEAD_PAD]
            b = qq[:, width + hd * HEAD_PAD:width + (hd + 1) * HEAD_PAD]
            q_ref[0, hd] = (a * cq_t + b * sq_t).astype(BF16)


def _proj(x, ctx, mods, tab, watt, gq, gkv, wq, wk, wvt):
    bsz, n, d = x.shape
    n_ctx = ctx.shape[1]
    assert n_ctx == ROW_TILE and n % ROW_TILE == 0
    nt = n // ROW_TILE
    n_keys = n_ctx + n
    const = lambda b, j: (0, 0)
    one = pl.Buffered(1)
    return pl.pallas_call(
        _proj_kernel,
        grid=(bsz, nt + 1),
        in_specs=[
            pl.BlockSpec((1, ROW_TILE, d), lambda b, j: (b, jnp.maximum(j - 1, 0), 0)),
            pl.BlockSpec((1, ROW_TILE, d), lambda b, j: (b, 0, 0)),
            pl.BlockSpec((1, SUBLANES, d), lambda b, j: (jnp.where(j == 0, bsz, b), 0, 0)),
            pl.BlockSpec((ROW_TILE, 4 * LANES), lambda b, j: (jnp.maximum(j - 1, 0), 0)),
            pl.BlockSpec(watt.shape, const, pipeline_mode=one),
            pl.BlockSpec(gq.shape, const, pipeline_mode=one),
            pl.BlockSpec(gkv.shape, const, pipeline_mode=one),
            pl.BlockSpec(wq.shape, const, pipeline_mode=one),
            pl.BlockSpec(wk.shape, const, pipeline_mode=one),
            pl.BlockSpec(wvt.shape, const, pipeline_mode=one),
        ],
        out_specs=[
            pl.BlockSpec((1, MLA_HEADS, ROW_TILE, HEAD_PAD), lambda b, j: (b, 0, jnp.maximum(j - 1, 0), 0)),
            pl.BlockSpec((1, MLA_HEADS, ROW_TILE, HEAD_PAD), lambda b, j: (b, 0, j, 0)),
            pl.BlockSpec((1, MLA_HEADS, V_ROWS, ROW_TILE), lambda b, j: (b, 0, 0, j)),
        ],
        out_shape=[
            jax.ShapeDtypeStruct((bsz, MLA_HEADS, n, HEAD_PAD), BF16),
            jax.ShapeDtypeStruct((bsz, MLA_HEADS, n_keys, HEAD_PAD), BF16),
            jax.ShapeDtypeStruct((bsz, MLA_HEADS, V_ROWS, n_keys), BF16),
        ],
        compiler_params=pltpu.CompilerParams(
            dimension_semantics=("arbitrary", "arbitrary"), vmem_limit_bytes=VMEM_LIMIT),
        name="proj",
    )(x, ctx, mods, tab, watt, gq, gkv, wq, wk, wvt)


def _attn_kernel(q_ref, k_ref, vt_ref, o_ref, s_ref):
    heads = q_ref.shape[1]
    nq = q_ref.shape[2] // Q_TILE

    def scores(h, i, slot):
        q = q_ref[0, h, i * Q_TILE:(i + 1) * Q_TILE, :]
        s_ref[slot] = _dot_nt(k_ref[0, h], q)

    def finish(h, i, slot):
        s_t = s_ref[slot]
        m = jnp.max(s_t, axis=0, keepdims=True)
        p_t = jnp.exp2(s_t - m).astype(BF16)
        o_t = _dot(vt_ref[0, h], p_t)
        o_ref[0, h, i] = o_t[0:V_DIM] * (1.0 / o_t[V_DIM:V_DIM + 1])

    def head(h, next_h):
        for i in range(nq):
            slot = i % 2
            if i + 1 < nq:
                scores(h, i + 1, 1 - slot)
            elif next_h is not None:
                scores(next_h, 0, 1 - slot)
            finish(h, i, slot)

    assert nq % 2 == 0
    scores(0, 0, 0)

    def body(h, carry):
        head(h, h + 1)
        return carry

    lax.fori_loop(0, heads - 1, body, 0)
    head(heads - 1, None)


def _attn(q, k, vt):
    bsz, heads, n, _ = q.shape
    n_keys = k.shape[2]
    nq = n // Q_TILE
    return pl.pallas_call(
        _attn_kernel,
        grid=(bsz,),
        in_specs=[
            pl.BlockSpec((1, heads, n, HEAD_PAD), lambda b: (b, 0, 0, 0)),
            pl.BlockSpec((1, heads, n_keys, HEAD_PAD), lambda b: (b, 0, 0, 0)),
            pl.BlockSpec((1, heads, V_ROWS, n_keys), lambda b: (b, 0, 0, 0)),
        ],
        out_specs=pl.BlockSpec((1, heads, nq, V_DIM, Q_TILE), lambda b: (b, 0, 0, 0, 0)),
        out_shape=jax.ShapeDtypeStruct((bsz, heads, nq, V_DIM, Q_TILE), F32),
        scratch_shapes=[pltpu.VMEM((2, n_keys, Q_TILE), F32)],
        compiler_params=pltpu.CompilerParams(
            dimension_semantics=("arbitrary",), vmem_limit_bytes=VMEM_LIMIT),
        name="attn",
    )(q, k, vt)


def _mixer_kernel(x_ref, xp_ref, xn_ref, mod_ref, att_ref, wt_ref, cw_ref, woc_ref, wom_ref, wo_ref, ln_ref,
                  o_ref):
    j = pl.program_id(1)
    last = pl.num_programs(1) - 1
    d = x_ref.shape[2]
    x = x_ref[0]
    mod = mod_ref[0]
    shift = mod[0:1]
    scale1 = 1.0 + mod[1:2]
    gate = mod[2:3]
    hx = (x * scale1 + shift).astype(BF16)

    def proj(h, off, width):
        return _dot_nt(h, wt_ref[off:off + width, :])

    u = proj(hx, O_XC, D_CONV) * proj(hx, O_CC, D_CONV)
    xh = jnp.concatenate([xp_ref[0], xn_ref[0]], axis=0)
    hh = (xh * scale1 + shift).astype(BF16)
    uh = proj(hh, O_XC, D_CONV) * proj(hh, O_CC, D_CONV)
    prev_edge = uh[SUBLANES - 1:SUBLANES] * (j > 0).astype(F32)
    next_edge = uh[SUBLANES:SUBLANES + 1] * (j < last).astype(F32)
    rows = u.shape[0]
    r = lax.broadcasted_iota(jnp.int32, u.shape, 0)
    u_prev = jnp.where(r == 0, prev_edge, pltpu.roll(u, 1, 0))
    u_next = jnp.where(r == rows - 1, next_edge, pltpu.roll(u, rows - 1, 0))
    cw = cw_ref[...]
    conv = u_prev * cw[0:1] + u * cw[1:2] + u_next * cw[2:3]

    zc = _silu(proj(hx, O_GC, D_CONV)) * proj(hx, O_BC, D_CONV) * conv
    yc = _dot(zc.astype(BF16), woc_ref[...])

    att = att_ref[0, :, 0].reshape(D_MLA, rows).T
    ym = _dot((_silu(proj(hx, O_GM, D_MLA)) * att).astype(BF16), wom_ref[...])

    merged = (jax.nn.sigmoid(proj(hx, O_GCONV, d)) * yc
              + jax.nn.sigmoid(proj(hx, O_GCONV + d, d)) * ym)
    y = _dot(merged.astype(BF16), wo_ref[...])

    res = DEEPNORM_ALPHA * x + gate * y
    mu = jnp.mean(res, axis=-1, keepdims=True)
    dev = res - mu
    var = jnp.mean(dev * dev, axis=-1, keepdims=True)
    ln = ln_ref[...]
    o_ref[0] = dev * lax.rsqrt(var + LN_EPS) * ln[0:1] + ln[1:2]


def _mixer(x, mods, att_t, wt, cw, woc, wom, wo, ln):
    bsz, n, d = x.shape
    assert MIX_TILE == Q_TILE and n % MIX_TILE == 0
    nt = n // MIX_TILE
    blocks8 = MIX_TILE // SUBLANES
    const = lambda b, j: (0, 0)
    one = pl.Buffered(1)
    return pl.pallas_call(
        _mixer_kernel,
        grid=(bsz, nt),
        in_specs=[
            pl.BlockSpec((1, MIX_TILE, d), lambda b, j: (b, j, 0)),
            pl.BlockSpec((1, SUBLANES, d), lambda b, j: (b, jnp.maximum(j * blocks8 - 1, 0), 0)),
            pl.BlockSpec((1, SUBLANES, d), lambda b, j: (b, jnp.minimum((j + 1) * blocks8, n // SUBLANES - 1), 0)),
            pl.BlockSpec((1, SUBLANES, d), lambda b, j: (b, 0, 0)),
            pl.BlockSpec((1, MLA_HEADS, 1, V_DIM, MIX_TILE), lambda b, j: (b, 0, j, 0, 0)),
            pl.BlockSpec(wt.shape, const, pipeline_mode=one),
            pl.BlockSpec(cw.shape, const, pipeline_mode=one),
            pl.BlockSpec(woc.shape, const, pipeline_mode=one),
            pl.BlockSpec(wom.shape, const, pipeline_mode=one),
            pl.BlockSpec(wo.shape, const, pipeline_mode=one),
            pl.BlockSpec(ln.shape, const, pipeline_mode=one),
        ],
        out_specs=pl.BlockSpec((1, MIX_TILE, d), lambda b, j: (b, j, 0)),
        out_shape=jax.ShapeDtypeStruct((bsz, n, d), F32),
        compiler_params=pltpu.CompilerParams(
            dimension_semantics=("arbitrary", "arbitrary"), vmem_limit_bytes=VMEM_LIMIT),
        name="mixer",
    )(x, x, x, mods, att_t, wt, cw, woc, wom, wo, ln)


def _rope_pair_swap():
    q = QK_ROPE // 4
    idx = np.concatenate([np.arange(q, 2 * q), np.arange(0, q), np.arange(3 * q, 4 * q), np.arange(2 * q, 3 * q)])
    sign = np.concatenate([-np.ones(q), np.ones(q), -np.ones(q), np.ones(q)]).astype(np.float32)
    return idx, sign


def _rope_tables(n, scale):
    rows = n // GRID_W
    row_pos = np.repeat(np.arange(rows), GRID_W).astype(np.float64)
    col_pos = np.tile(np.arange(GRID_W), rows).astype(np.float64)
    axis_dim = QK_ROPE // 2
    inv_freq = ROPE_THETA ** (-np.arange(0, axis_dim, 2, dtype=np.float64) / axis_dim)
    ang_r = row_pos[:, None] * inv_freq[None, :]
    ang_c = col_pos[:, None] * inv_freq[None, :]
    cos = np.concatenate([np.cos(ang_r), np.cos(ang_r), np.cos(ang_c), np.cos(ang_c)], axis=-1)
    sin = np.concatenate([np.sin(ang_r), np.sin(ang_r), np.sin(ang_c), np.sin(ang_c)], axis=-1)
    pad = np.zeros((n, HEAD_PAD - QK_NOPE - QK_ROPE))
    zero_n = np.zeros((n, QK_NOPE))
    cq = np.concatenate([np.full((n, QK_NOPE), scale), cos * scale, pad], axis=-1)
    sq = np.concatenate([zero_n, sin * scale, pad], axis=-1)
    ck = np.concatenate([zero_n, cos, pad], axis=-1)
    sk = np.concatenate([zero_n, sin, pad], axis=-1)
    return jnp.asarray(np.concatenate([cq, sq, ck, sk], axis=-1), dtype=F32)


def _place_rope_rows(w):
    d = w.shape[1]
    return jnp.concatenate([jnp.zeros((QK_NOPE, d), w.dtype), w,
                            jnp.zeros((HEAD_PAD - QK_NOPE - QK_ROPE, d), w.dtype)], axis=0)


def kernel(x, c, ctx, c_ctx, w_ada, b_ada, w_in, conv_w, q_norm_g, w_uq, kv_norm_g, w_ukv, w_out_conv,
           w_out_mla, w_o, ln_g, ln_b):
    bsz, n, d = x.shape
    assert w_ada.shape[0] == DEPTH == 1
    w_ada, b_ada, w_in, conv_w = w_ada[0], b_ada[0], w_in[0], conv_w[0]
    q_norm_g, w_uq, kv_norm_g, w_ukv = q_norm_g[0], w_uq[0], kv_norm_g[0], w_ukv[0]
    w_out_conv, w_out_mla, w_o, ln_g, ln_b = w_out_conv[0], w_out_mla[0], w_o[0], ln_g[0], ln_b[0]

    pad_rows = (-(bsz + 1)) % SUBLANES
    c_rows = jnp.concatenate([c, c_ctx[None, :], jnp.zeros((pad_rows, d), F32)], axis=0)
    mod = _adaln(c_rows, w_ada, b_ada[None, :])
    mods = mod[:bsz + 1].reshape(bsz + 1, 3, d)
    mods = jnp.concatenate([mods, jnp.zeros((bsz + 1, SUBLANES - 3, d), F32)], axis=1)

    wt = jnp.swapaxes(w_in, 0, 1).astype(BF16)

    idx, sign = _rope_pair_swap()
    w_kr = wt[O_KR:O_KR + QK_ROPE]
    w_kr_rot = w_kr[idx] * sign[:, None].astype(BF16)
    watt = jnp.concatenate([wt[O_CQ:O_KR], _place_rope_rows(w_kr), _place_rope_rows(w_kr_rot)], axis=0)

    hq = QK_NOPE + QK_ROPE
    w_uq_h = w_uq.reshape(Q_LORA, MLA_HEADS, hq)
    zq = jnp.zeros((Q_LORA, MLA_HEADS, HEAD_PAD - hq), F32)
    wq_plain = jnp.concatenate([w_uq_h, zq], axis=-1).reshape(Q_LORA, MLA_HEADS * HEAD_PAD)
    wq_rot = jnp.concatenate([jnp.zeros((Q_LORA, MLA_HEADS, QK_NOPE), F32),
                              w_uq_h[..., QK_NOPE + idx] * sign, zq], axis=-1).reshape(Q_LORA, MLA_HEADS * HEAD_PAD)
    wq = jnp.concatenate([wq_plain, wq_rot], axis=-1).astype(BF16)

    w_ukv_h = w_ukv.reshape(KV_LORA, MLA_HEADS, QK_NOPE + V_DIM)
    wk = jnp.concatenate([w_ukv_h[..., :QK_NOPE], jnp.zeros((KV_LORA, MLA_HEADS, HEAD_PAD - QK_NOPE), F32)],
                         axis=-1).reshape(KV_LORA, MLA_HEADS * HEAD_PAD).astype(BF16)
    wv = jnp.concatenate([w_ukv_h[..., QK_NOPE:], jnp.zeros((KV_LORA, MLA_HEADS, V_ROWS - V_DIM), F32)], axis=-1)
    wvt = wv.reshape(KV_LORA, MLA_HEADS * V_ROWS).T.astype(BF16)

    tab = _rope_tables(n, float((QK_NOPE + QK_ROPE) ** -0.5 * np.log2(np.e)))
    q, k, vt = _proj(x, ctx, mods, tab, watt, q_norm_g[None, :], kv_norm_g[None, :], wq, wk, wvt)
    att_t = _attn(q, k, vt)

    cw = jnp.concatenate([conv_w, jnp.zeros((SUBLANES - conv_w.shape[0], D_CONV), F32)], axis=0)
    ln = jnp.concatenate([ln_g[None, :], ln_b[None, :], jnp.zeros((SUBLANES - 2, d), F32)], axis=0)
    return _mixer(x, mods, att_t, wt, cw, w_out_conv.astype(BF16), w_out_mla.astype(BF16), w_o.astype(BF16), ln)
```

```python
import jax
import jax.numpy as jnp
import numpy as np
from jax import lax
from jax.experimental import pallas as pl
from jax.experimental.pallas import tpu as pltpu

GRID_W = 64
D_CONV = 512
MLA_HEADS = 8
QK_NOPE = 64
QK_ROPE = 32
V_DIM = 64
D_MLA = MLA_HEADS * V_DIM
Q_LORA = 256
KV_LORA = 128
ROPE_THETA = 10000.0
LN_EPS = 1e-5
RMS_EPS = 1e-6
DEPTH = 1
DEEPNORM_ALPHA = (2.0 * DEPTH) ** 0.25

O_XC, O_BC, O_CC, O_GC = 0, D_CONV, 2 * D_CONV, 3 * D_CONV
O_CQ = 4 * D_CONV
O_CKV = O_CQ + Q_LORA
O_KR = O_CKV + KV_LORA
O_GM = O_KR + QK_ROPE
O_GCONV = O_GM + D_MLA

LANES = 128
SUBLANES = 8
HEAD_PAD = LANES
V_ROWS = V_DIM + 16
PROJ_TILE = 512
MIX_TILE = 512
Q_TILE = 512
VMEM_LIMIT = 56 * 1024 * 1024

F32 = jnp.float32
BF16 = jnp.bfloat16


def _dot(a, b):
    return jnp.dot(a, b, preferred_element_type=F32)


def _dot_nt(a, b):
    return lax.dot_general(a, b, (((1,), (1,)), ((), ())), preferred_element_type=F32)


def _silu(t):
    return t * jax.nn.sigmoid(t)


def _rms_norm(t, g):
    return t * lax.rsqrt(jnp.mean(t * t, axis=-1, keepdims=True) + RMS_EPS) * g


def _adaln_kernel(c_ref, w_ref, b_ref, o_ref):
    cs = _silu(c_ref[...])
    o_ref[...] = jnp.dot(cs, w_ref[...], precision=lax.Precision.HIGHEST,
                         preferred_element_type=F32) + b_ref[...]


def _adaln(c_rows, w_ada, b_ada):
    rows, d = c_rows.shape
    n_out = w_ada.shape[1]
    col_tile = 512
    return pl.pallas_call(
        _adaln_kernel,
        grid=(n_out // col_tile,),
        in_specs=[
            pl.BlockSpec((rows, d), lambda i: (0, 0)),
            pl.BlockSpec((d, col_tile), lambda i: (0, i)),
            pl.BlockSpec((1, col_tile), lambda i: (0, i)),
        ],
        out_specs=pl.BlockSpec((rows, col_tile), lambda i: (0, i)),
        out_shape=jax.ShapeDtypeStruct((rows, n_out), F32),
        name="adaln",
    )(c_rows, w_ada, b_ada)


def _keys_values(ckv, kr_placed, gkv_ref, wk_ref, wvt_ref, k_ref, vt_ref):
    kvn = _rms_norm(ckv, gkv_ref[...]).astype(BF16)
    kn = _dot(kvn, wk_ref[...])
    for h in range(MLA_HEADS):
        k_ref[0, h] = (kn[:, h * HEAD_PAD:(h + 1) * HEAD_PAD] + kr_placed).astype(BF16)
    vt = _dot_nt(wvt_ref[...], kvn)
    r = lax.broadcasted_iota(jnp.int32, vt.shape, 0)
    ones = r == V_DIM
    for h in range(1, MLA_HEADS):
        ones = ones | (r == h * V_ROWS + V_DIM)
    vt = jnp.where(ones, 1.0, vt)
    vt_ref[0] = vt.astype(BF16).reshape(MLA_HEADS, V_ROWS, vt.shape[1])


def _proj_x_kernel(x_ref, mod_ref, tab_ref, watt_ref, gq_ref, gkv_ref, wq_ref, wk_ref, wvt_ref,
                   q_ref, k_ref, vt_ref):
    mod = mod_ref[0]
    h = (x_ref[0] * (1.0 + mod[1:2]) + mod[0:1]).astype(BF16)
    p = _dot_nt(h, watt_ref[...])
    tab = tab_ref[...]
    cq_t, sq_t = tab[:, 0:LANES], tab[:, LANES:2 * LANES]
    ck_t, sk_t = tab[:, 2 * LANES:3 * LANES], tab[:, 3 * LANES:4 * LANES]
    kr0 = Q_LORA + KV_LORA
    kr_placed = p[:, kr0:kr0 + LANES] * ck_t + p[:, kr0 + LANES:kr0 + 2 * LANES] * sk_t
    _keys_values(p[:, Q_LORA:kr0], kr_placed, gkv_ref, wk_ref, wvt_ref, k_ref, vt_ref)
    qn = _rms_norm(p[:, 0:Q_LORA], gq_ref[...]).astype(BF16)
    qq = _dot(qn, wq_ref[...])
    width = MLA_HEADS * HEAD_PAD
    for hd in range(MLA_HEADS):
        a = qq[:, hd * HEAD_PAD:(hd + 1) * HEAD_PAD]
        b = qq[:, width + hd * HEAD_PAD:width + (hd + 1) * HEAD_PAD]
        q_ref[0, hd] = (a * cq_t + b * sq_t).astype(BF16)


def _proj_ctx_kernel(ctx_ref, mod_ref, watt_ref, gkv_ref, wk_ref, wvt_ref, k_ref, vt_ref):
    mod = mod_ref[0]
    h = (ctx_ref[0] * (1.0 + mod[1:2]) + mod[0:1]).astype(BF16)
    p = _dot_nt(h, watt_ref[Q_LORA:Q_LORA + KV_LORA + LANES, :])
    _keys_values(p[:, 0:KV_LORA], p[:, KV_LORA:], gkv_ref, wk_ref, wvt_ref, k_ref, vt_ref)


def _proj_x(x, mods, tab, watt, gq, gkv, wq, wk, wvt):
    bsz, n, d = x.shape
    assert n % PROJ_TILE == 0
    const = lambda b, j: (0, 0)
    one = pl.Buffered(1)
    head_block = pl.BlockSpec((1, MLA_HEADS, PROJ_TILE, HEAD_PAD), lambda b, j: (b, 0, j, 0))
    return pl.pallas_call(
        _proj_x_kernel,
        grid=(bsz, n // PROJ_TILE),
        in_specs=[
            pl.BlockSpec((1, PROJ_TILE, d), lambda b, j: (b, j, 0)),
            pl.BlockSpec((1, SUBLANES, d), lambda b, j: (b, 0, 0)),
            pl.BlockSpec((PROJ_TILE, 4 * LANES), lambda b, j: (j, 0)),
            pl.BlockSpec(watt.shape, const, pipeline_mode=one),
            pl.BlockSpec(gq.shape, const, pipeline_mode=one),
            pl.BlockSpec(gkv.shape, const, pipeline_mode=one),
            pl.BlockSpec(wq.shape, const, pipeline_mode=one),
            pl.BlockSpec(wk.shape, const, pipeline_mode=one),
            pl.BlockSpec(wvt.shape, const, pipeline_mode=one),
        ],
        out_specs=[head_block, head_block,
                   pl.BlockSpec((1, MLA_HEADS, V_ROWS, PROJ_TILE), lambda b, j: (b, 0, 0, j))],
        out_shape=[
            jax.ShapeDtypeStruct((bsz, MLA_HEADS, n, HEAD_PAD), BF16),
            jax.ShapeDtypeStruct((bsz, MLA_HEADS, n, HEAD_PAD), BF16),
            jax.ShapeDtypeStruct((bsz, MLA_HEADS, V_ROWS, n), BF16),
        ],
        compiler_params=pltpu.CompilerParams(
            dimension_semantics=("arbitrary", "arbitrary"), vmem_limit_bytes=VMEM_LIMIT),
        name="proj_x",
    )(x, mods, tab, watt, gq, gkv, wq, wk, wvt)


def _proj_ctx(ctx, mods, watt, gkv, wk, wvt):
    bsz, n_ctx, d = ctx.shape
    const = lambda b: (0, 0)
    one = pl.Buffered(1)
    return pl.pallas_call(
        _proj_ctx_kernel,
        grid=(bsz,),
        in_specs=[
            pl.BlockSpec((1, n_ctx, d), lambda b: (b, 0, 0)),
            pl.BlockSpec((1, SUBLANES, d), lambda b: (bsz, 0, 0)),
            pl.BlockSpec(watt.shape, const, pipeline_mode=one),
            pl.BlockSpec(gkv.shape, const, pipeline_mode=one),
            pl.BlockSpec(wk.shape, const, pipeline_mode=one),
            pl.BlockSpec(wvt.shape, const, pipeline_mode=one),
        ],
        out_specs=[pl.BlockSpec((1, MLA_HEADS, n_ctx, HEAD_PAD), lambda b: (b, 0, 0, 0)),
                   pl.BlockSpec((1, MLA_HEADS, V_ROWS, n_ctx), lambda b: (b, 0, 0, 0))],
        out_shape=[
            jax.ShapeDtypeStruct((bsz, MLA_HEADS, n_ctx, HEAD_PAD), BF16),
            jax.ShapeDtypeStruct((bsz, MLA_HEADS, V_ROWS, n_ctx), BF16),
        ],
        compiler_params=pltpu.CompilerParams(dimension_semantics=("arbitrary",), vmem_limit_bytes=VMEM_LIMIT),
        name="proj_ctx",
    )(ctx, mods, watt, gkv, wk, wvt)


def _attn_kernel(q_ref, kx_ref, kc_ref, vtx_ref, vtc_ref, o_ref, s_ref):
    heads = q_ref.shape[1]
    nq = q_ref.shape[2] // Q_TILE
    nx = kx_ref.shape[2]

    def scores(h, i, slot):
        q = q_ref[0, h, i * Q_TILE:(i + 1) * Q_TILE, :]
        s_ref[slot, 0:nx] = _dot_nt(kx_ref[0, h], q)
        s_ref[slot, nx:] = _dot_nt(kc_ref[0, h], q)

    def finish(h, i, slot):
        s_t = s_ref[slot]
        m = jnp.max(s_t, axis=0, keepdims=True)
        p_t = jnp.exp2(s_t - m).astype(BF16)
        o_t = _dot(vtx_ref[0, h], p_t[0:nx]) + _dot(vtc_ref[0, h], p_t[nx:])
        o_ref[0, h, i] = o_t[0:V_DIM] * (1.0 / o_t[V_DIM:V_DIM + 1])

    def head(h, next_h):
        for i in range(nq):
            slot = i % 2
            if i + 1 < nq:
                scores(h, i + 1, 1 - slot)
            elif next_h is not None:
                scores(next_h, 0, 1 - slot)
            finish(h, i, slot)

    assert nq % 2 == 0
    scores(0, 0, 0)

    def body(h, carry):
        head(h, h + 1)
        return carry

    lax.fori_loop(0, heads - 1, body, 0)
    head(heads - 1, None)


def _attn(q, kx, kc, vtx, vtc):
    bsz, heads, n, _ = q.shape
    n_ctx = kc.shape[2]
    n_keys = n + n_ctx
    nq = n // Q_TILE
    whole = lambda b: (b, 0, 0, 0)
    return pl.pallas_call(
        _attn_kernel,
        grid=(bsz,),
        in_specs=[
            pl.BlockSpec((1, heads, n, HEAD_PAD), whole),
            pl.BlockSpec((1, heads, n, HEAD_PAD), whole),
            pl.BlockSpec((1, heads, n_ctx, HEAD_PAD), whole),
            pl.BlockSpec((1, heads, V_ROWS, n), whole),
            pl.BlockSpec((1, heads, V_ROWS, n_ctx), whole),
        ],
        out_specs=pl.BlockSpec((1, heads, nq, V_DIM, Q_TILE), lambda b: (b, 0, 0, 0, 0)),
        out_shape=jax.ShapeDtypeStruct((bsz, heads, nq, V_DIM, Q_TILE), F32),
        scratch_shapes=[pltpu.VMEM((2, n_keys, Q_TILE), F32)],
        compiler_params=pltpu.CompilerParams(
            dimension_semantics=("arbitrary",), vmem_limit_bytes=VMEM_LIMIT),
        name="attn",
    )(q, kx, kc, vtx, vtc)


def _mixer_kernel(x_ref, xp_ref, xn_ref, mod_ref, att_ref, wt_ref, cw_ref, woc_ref, wom_ref, wo_ref, ln_ref,
                  o_ref):
    j = pl.program_id(1)
    last = pl.num_programs(1) - 1
    d = x_ref.shape[2]
    x = x_ref[0]
    mod = mod_ref[0]
    shift = mod[0:1]
    scale1 = 1.0 + mod[1:2]
    gate = mod[2:3]
    hx = (x * scale1 + shift).astype(BF16)

    def proj(h, off, width):
        return _dot_nt(h, wt_ref[off:off + width, :])

    u = proj(hx, O_XC, D_CONV) * proj(hx, O_CC, D_CONV)
    xh = jnp.concatenate([xp_ref[0], xn_ref[0]], axis=0)
    hh = (xh * scale1 + shift).astype(BF16)
    uh = proj(hh, O_XC, D_CONV) * proj(hh, O_CC, D_CONV)
    prev_edge = uh[SUBLANES - 1:SUBLANES] * (j > 0).astype(F32)
    next_edge = uh[SUBLANES:SUBLANES + 1] * (j < last).astype(F32)
    rows = u.shape[0]
    r = lax.broadcasted_iota(jnp.int32, u.shape, 0)
    u_prev = jnp.where(r == 0, prev_edge, pltpu.roll(u, 1, 0))
    u_next = jnp.where(r == rows - 1, next_edge, pltpu.roll(u, rows - 1, 0))
    cw = cw_ref[...]
    conv = u_prev * cw[0:1] + u * cw[1:2] + u_next * cw[2:3]

    zc = _silu(proj(hx, O_GC, D_CONV)) * proj(hx, O_BC, D_CONV) * conv
    yc = _dot(zc.astype(BF16), woc_ref[...])

    att = att_ref[0, :, 0].reshape(D_MLA, rows).T
    ym = _dot((_silu(proj(hx, O_GM, D_MLA)) * att).astype(BF16), wom_ref[...])

    merged = (jax.nn.sigmoid(proj(hx, O_GCONV, d)) * yc
              + jax.nn.sigmoid(proj(hx, O_GCONV + d, d)) * ym)
    y = _dot(merged.astype(BF16), wo_ref[...])

    res = DEEPNORM_ALPHA * x + gate * y
    mu = jnp.mean(res, axis=-1, keepdims=True)
    dev = res - mu
    var = jnp.mean(dev * dev, axis=-1, keepdims=True)
    ln = ln_ref[...]
    o_ref[0] = dev * lax.rsqrt(var + LN_EPS) * ln[0:1] + ln[1:2]


def _mixer(x, mods, att_t, wt, cw, woc, wom, wo, ln):
    bsz, n, d = x.shape
    assert MIX_TILE == Q_TILE and n % MIX_TILE == 0
    nt = n // MIX_TILE
    blocks8 = MIX_TILE // SUBLANES
    const = lambda b, j: (0, 0)
    one = pl.Buffered(1)
    return pl.pallas_call(
        _mixer_kernel,
        grid=(bsz, nt),
        in_specs=[
            pl.BlockSpec((1, MIX_TILE, d), lambda b, j: (b, j, 0)),
            pl.BlockSpec((1, SUBLANES, d), lambda b, j: (b, jnp.maximum(j * blocks8 - 1, 0), 0)),
            pl.BlockSpec((1, SUBLANES, d), lambda b, j: (b, jnp.minimum((j + 1) * blocks8, n // SUBLANES - 1), 0)),
            pl.BlockSpec((1, SUBLANES, d), lambda b, j: (b, 0, 0)),
            pl.BlockSpec((1, MLA_HEADS, 1, V_DIM, MIX_TILE), lambda b, j: (b, 0, j, 0, 0)),
            pl.BlockSpec(wt.shape, const, pipeline_mode=one),
            pl.BlockSpec(cw.shape, const, pipeline_mode=one),
            pl.BlockSpec(woc.shape, const, pipeline_mode=one),
            pl.BlockSpec(wom.shape, const, pipeline_mode=one),
            pl.BlockSpec(wo.shape, const, pipeline_mode=one),
            pl.BlockSpec(ln.shape, const, pipeline_mode=one),
        ],
        out_specs=pl.BlockSpec((1, MIX_TILE, d), lambda b, j: (b, j, 0)),
        out_shape=jax.ShapeDtypeStruct((bsz, n, d), F32),
        compiler_params=pltpu.CompilerParams(
            dimension_semantics=("arbitrary", "arbitrary"), vmem_limit_bytes=VMEM_LIMIT),
        name="mixer",
    )(x, x, x, mods, att_t, wt, cw, woc, wom, wo, ln)


def _rope_pair_swap():
    q = QK_ROPE // 4
    idx = np.concatenate([np.arange(q, 2 * q), np.arange(0, q), np.arange(3 * q, 4 * q), np.arange(2 * q, 3 * q)])
    sign = np.concatenate([-np.ones(q), np.ones(q), -np.ones(q), np.ones(q)]).astype(np.float32)
    return idx, sign


def _rope_tables(n, scale):
    rows = n // GRID_W
    row_pos = np.repeat(np.arange(rows), GRID_W).astype(np.float64)
    col_pos = np.tile(np.arange(GRID_W), rows).astype(np.float64)
    axis_dim = QK_ROPE // 2
    inv_freq = ROPE_THETA ** (-np.arange(0, axis_dim, 2, dtype=np.float64) / axis_dim)
    ang_r = row_pos[:, None] * inv_freq[None, :]
    ang_c = col_pos[:, None] * inv_freq[None, :]
    cos = np.concatenate([np.cos(ang_r), np.cos(ang_r), np.cos(ang_c), np.cos(ang_c)], axis=-1)
    sin = np.concatenate([np.sin(ang_r), np.sin(ang_r), np.sin(ang_c), np.sin(ang_c)], axis=-1)
    pad = np.zeros((n, HEAD_PAD - QK_NOPE - QK_ROPE))
    zero_n = np.zeros((n, QK_NOPE))
    cq = np.concatenate([np.full((n, QK_NOPE), scale), cos * scale, pad], axis=-1)
    sq = np.concatenate([zero_n, sin * scale, pad], axis=-1)
    ck = np.concatenate([zero_n, cos, pad], axis=-1)
    sk = np.concatenate([zero_n, sin, pad], axis=-1)
    return jnp.asarray(np.concatenate([cq, sq, ck, sk], axis=-1), dtype=F32)


def _place_rope_rows(w):
    d = w.shape[1]
    return jnp.concatenate([jnp.zeros((QK_NOPE, d), w.dtype), w,
                            jnp.zeros((HEAD_PAD - QK_NOPE - QK_ROPE, d), w.dtype)], axis=0)


def kernel(x, c, ctx, c_ctx, w_ada, b_ada, w_in, conv_w, q_norm_g, w_uq, kv_norm_g, w_ukv, w_out_conv,
           w_out_mla, w_o, ln_g, ln_b):
    bsz, n, d = x.shape
    assert w_ada.shape[0] == DEPTH == 1
    w_ada, b_ada, w_in, conv_w = w_ada[0], b_ada[0], w_in[0], conv_w[0]
    q_norm_g, w_uq, kv_norm_g, w_ukv = q_norm_g[0], w_uq[0], kv_norm_g[0], w_ukv[0]
    w_out_conv, w_out_mla, w_o, ln_g, ln_b = w_out_conv[0], w_out_mla[0], w_o[0], ln_g[0], ln_b[0]

    pad_rows = (-(bsz + 1)) % SUBLANES
    c_rows = jnp.concatenate([c, c_ctx[None, :], jnp.zeros((pad_rows, d), F32)], axis=0)
    mod = _adaln(c_rows, w_ada, b_ada[None, :])
    mods = mod[:bsz + 1].reshape(bsz + 1, 3, d)
    mods = jnp.concatenate([mods, jnp.zeros((bsz + 1, SUBLANES - 3, d), F32)], axis=1)

    wt = jnp.swapaxes(w_in, 0, 1).astype(BF16)

    idx, sign = _rope_pair_swap()
    w_kr = wt[O_KR:O_KR + QK_ROPE]
    w_kr_rot = w_kr[idx] * sign[:, None].astype(BF16)
    watt = jnp.concatenate([wt[O_CQ:O_KR], _place_rope_rows(w_kr), _place_rope_rows(w_kr_rot)], axis=0)

    hq = QK_NOPE + QK_ROPE
    w_uq_h = w_uq.reshape(Q_LORA, MLA_HEADS, hq)
    zq = jnp.zeros((Q_LORA, MLA_HEADS, HEAD_PAD - hq), F32)
    wq_plain = jnp.concatenate([w_uq_h, zq], axis=-1).reshape(Q_LORA, MLA_HEADS * HEAD_PAD)
    wq_rot = jnp.concatenate([jnp.zeros((Q_LORA, MLA_HEADS, QK_NOPE), F32),
                              w_uq_h[..., QK_NOPE + idx] * sign, zq], axis=-1).reshape(Q_LORA, MLA_HEADS * HEAD_PAD)
    wq = jnp.concatenate([wq_plain, wq_rot], axis=-1).astype(BF16)

    w_ukv_h = w_ukv.reshape(KV_LORA, MLA_HEADS, QK_NOPE + V_DIM)
    wk = jnp.concatenate([w_ukv_h[..., :QK_NOPE], jnp.zeros((KV_LORA, MLA_HEADS, HEAD_PAD - QK_NOPE), F32)],
                         axis=-1).reshape(KV_LORA, MLA_HEADS * HEAD_PAD).astype(BF16)
    wv = jnp.concatenate([w_ukv_h[..., QK_NOPE:], jnp.zeros((KV_LORA, MLA_HEADS, V_ROWS - V_DIM), F32)], axis=-1)
    wvt = wv.reshape(KV_LORA, MLA_HEADS * V_ROWS).T.astype(BF16)

    tab = _rope_tables(n, float((QK_NOPE + QK_ROPE) ** -0.5 * np.log2(np.e)))
    gq, gkv = q_norm_g[None, :], kv_norm_g[None, :]
    q, kx, vtx = _proj_x(x, mods, tab, watt, gq, gkv, wq, wk, wvt)
    kc, vtc = _proj_ctx(ctx, mods, watt, gkv, wk, wvt)
    att_t = _attn(q, kx, kc, vtx, vtc)

    cw = jnp.concatenate([conv_w, jnp.zeros((SUBLANES - conv_w.shape[0], D_CONV), F32)], axis=0)
    ln = jnp.concatenate([ln_g[None, :], ln_b[None, :], jnp.zeros((SUBLANES - 2, d), F32)], axis=0)
    return _mixer(x, mods, att_t, wt, cw, w_out_conv.astype(BF16), w_out_mla.astype(BF16), w_o.astype(BF16), ln)
```

```python
import functools

import jax
import jax.numpy as jnp
import numpy as np
from jax import lax
from jax.experimental import pallas as pl
from jax.experimental.pallas import tpu as pltpu

GRID_W = 64
D_CONV = 512
MLA_HEADS = 8
QK_NOPE = 64
QK_ROPE = 32
V_DIM = 64
D_MLA = MLA_HEADS * V_DIM
Q_LORA = 256
KV_LORA = 128
ROPE_THETA = 10000.0
LN_EPS = 1e-5
RMS_EPS = 1e-6
DEPTH = 1
DEEPNORM_ALPHA = (2.0 * DEPTH) ** 0.25

O_XC, O_BC, O_CC, O_GC = 0, D_CONV, 2 * D_CONV, 3 * D_CONV
O_CQ = 4 * D_CONV
O_CKV = O_CQ + Q_LORA
O_KR = O_CKV + KV_LORA
O_GM = O_KR + QK_ROPE
O_GCONV = O_GM + D_MLA

LANES = 128
SUBLANES = 8
HEAD_PAD = LANES
V_ROWS = V_DIM + 16
PROJ_TILE = 512
MIX_TILE = 512
Q_TILE = 512
VMEM_LIMIT = 56 * 1024 * 1024

F32 = jnp.float32
BF16 = jnp.bfloat16


def _dot(a, b):
    return jnp.dot(a, b, preferred_element_type=F32)


def _dot_nt(a, b):
    return lax.dot_general(a, b, (((1,), (1,)), ((), ())), preferred_element_type=F32)


def _silu(t):
    return t * jax.nn.sigmoid(t)


def _rms_norm(t, g):
    return t * lax.rsqrt(jnp.mean(t * t, axis=-1, keepdims=True) + RMS_EPS) * g


def _adaln_kernel(c_ref, w_ref, b_ref, o_ref):
    cs = _silu(c_ref[...])
    o_ref[...] = jnp.dot(cs, w_ref[...], precision=lax.Precision.HIGHEST,
                         preferred_element_type=F32) + b_ref[...]


def _adaln(c_rows, w_ada, b_ada):
    rows, d = c_rows.shape
    n_out = w_ada.shape[1]
    col_tile = 512
    return pl.pallas_call(
        _adaln_kernel,
        grid=(n_out // col_tile,),
        in_specs=[
            pl.BlockSpec((rows, d), lambda i: (0, 0)),
            pl.BlockSpec((d, col_tile), lambda i: (0, i)),
            pl.BlockSpec((1, col_tile), lambda i: (0, i)),
        ],
        out_specs=pl.BlockSpec((rows, col_tile), lambda i: (0, i)),
        out_shape=jax.ShapeDtypeStruct((rows, n_out), F32),
        name="adaln",
    )(c_rows, w_ada, b_ada)


def _keys_values(ckv, kr_placed, gkv_ref, wk_ref, wvt_ref, k_ref, vt_ref):
    kvn = _rms_norm(ckv, gkv_ref[...]).astype(BF16)
    kn = _dot(kvn, wk_ref[...])
    for h in range(MLA_HEADS):
        k_ref[0, h] = (kn[:, h * HEAD_PAD:(h + 1) * HEAD_PAD] + kr_placed).astype(BF16)
    vt = _dot_nt(wvt_ref[...], kvn)
    r = lax.broadcasted_iota(jnp.int32, vt.shape, 0)
    ones = r == V_DIM
    for h in range(1, MLA_HEADS):
        ones = ones | (r == h * V_ROWS + V_DIM)
    vt = jnp.where(ones, 1.0, vt)
    vt_ref[0] = vt.astype(BF16).reshape(MLA_HEADS, V_ROWS, vt.shape[1])


def _proj_x_kernel(x_ref, mod_ref, tab_ref, watt_ref, gq_ref, gkv_ref, wq_ref, wk_ref, wvt_ref,
                   q_ref, k_ref, vt_ref):
    mod = mod_ref[0]
    h = (x_ref[0] * (1.0 + mod[1:2]) + mod[0:1]).astype(BF16)
    p = _dot_nt(h, watt_ref[...])
    tab = tab_ref[...]
    cq_t, sq_t = tab[:, 0:LANES], tab[:, LANES:2 * LANES]
    ck_t, sk_t = tab[:, 2 * LANES:3 * LANES], tab[:, 3 * LANES:4 * LANES]
    kr0 = Q_LORA + KV_LORA
    kr_placed = p[:, kr0:kr0 + LANES] * ck_t + p[:, kr0 + LANES:kr0 + 2 * LANES] * sk_t
    _keys_values(p[:, Q_LORA:kr0], kr_placed, gkv_ref, wk_ref, wvt_ref, k_ref, vt_ref)
    qn = _rms_norm(p[:, 0:Q_LORA], gq_ref[...]).astype(BF16)
    qq = _dot(qn, wq_ref[...])
    width = MLA_HEADS * HEAD_PAD
    for hd in range(MLA_HEADS):
        a = qq[:, hd * HEAD_PAD:(hd + 1) * HEAD_PAD]
        b = qq[:, width + hd * HEAD_PAD:width + (hd + 1) * HEAD_PAD]
        q_ref[0, hd] = (a * cq_t + b * sq_t).astype(BF16)


def _proj_ctx_kernel(ctx_ref, mod_ref, watt_ref, gkv_ref, wk_ref, wvt_ref, k_ref, vt_ref):
    mod = mod_ref[0]
    h = (ctx_ref[0] * (1.0 + mod[1:2]) + mod[0:1]).astype(BF16)
    p = _dot_nt(h, watt_ref[Q_LORA:Q_LORA + KV_LORA + LANES, :])
    _keys_values(p[:, 0:KV_LORA], p[:, KV_LORA:], gkv_ref, wk_ref, wvt_ref, k_ref, vt_ref)


def _proj_x(x, mods, tab, watt, gq, gkv, wq, wk, wvt):
    bsz, n, d = x.shape
    assert n % PROJ_TILE == 0
    const = lambda b, j: (0, 0)
    one = pl.Buffered(1)
    head_block = pl.BlockSpec((1, MLA_HEADS, PROJ_TILE, HEAD_PAD), lambda b, j: (b, 0, j, 0))
    return pl.pallas_call(
        _proj_x_kernel,
        grid=(bsz, n // PROJ_TILE),
        in_specs=[
            pl.BlockSpec((1, PROJ_TILE, d), lambda b, j: (b, j, 0)),
            pl.BlockSpec((1, SUBLANES, d), lambda b, j: (b, 0, 0)),
            pl.BlockSpec((PROJ_TILE, 4 * LANES), lambda b, j: (j, 0)),
            pl.BlockSpec(watt.shape, const, pipeline_mode=one),
            pl.BlockSpec(gq.shape, const, pipeline_mode=one),
            pl.BlockSpec(gkv.shape, const, pipeline_mode=one),
            pl.BlockSpec(wq.shape, const, pipeline_mode=one),
            pl.BlockSpec(wk.shape, const, pipeline_mode=one),
            pl.BlockSpec(wvt.shape, const, pipeline_mode=one),
        ],
        out_specs=[head_block, head_block,
                   pl.BlockSpec((1, MLA_HEADS, V_ROWS, PROJ_TILE), lambda b, j: (b, 0, 0, j))],
        out_shape=[
            jax.ShapeDtypeStruct((bsz, MLA_HEADS, n, HEAD_PAD), BF16),
            jax.ShapeDtypeStruct((bsz, MLA_HEADS, n, HEAD_PAD), BF16),
            jax.ShapeDtypeStruct((bsz, MLA_HEADS, V_ROWS, n), BF16),
        ],
        compiler_params=pltpu.CompilerParams(
            dimension_semantics=("arbitrary", "arbitrary"), vmem_limit_bytes=VMEM_LIMIT),
        name="proj_x",
    )(x, mods, tab, watt, gq, gkv, wq, wk, wvt)


def _proj_ctx(ctx, mods, watt, gkv, wk, wvt):
    bsz, n_ctx, d = ctx.shape
    const = lambda b: (0, 0)
    one = pl.Buffered(1)
    return pl.pallas_call(
        _proj_ctx_kernel,
        grid=(bsz,),
        in_specs=[
            pl.BlockSpec((1, n_ctx, d), lambda b: (b, 0, 0)),
            pl.BlockSpec((1, SUBLANES, d), lambda b: (bsz, 0, 0)),
            pl.BlockSpec(watt.shape, const, pipeline_mode=one),
            pl.BlockSpec(gkv.shape, const, pipeline_mode=one),
            pl.BlockSpec(wk.shape, const, pipeline_mode=one),
            pl.BlockSpec(wvt.shape, const, pipeline_mode=one),
        ],
        out_specs=[pl.BlockSpec((1, MLA_HEADS, n_ctx, HEAD_PAD), lambda b: (b, 0, 0, 0)),
                   pl.BlockSpec((1, MLA_HEADS, V_ROWS, n_ctx), lambda b: (b, 0, 0, 0))],
        out_shape=[
            jax.ShapeDtypeStruct((bsz, MLA_HEADS, n_ctx, HEAD_PAD), BF16),
            jax.ShapeDtypeStruct((bsz, MLA_HEADS, V_ROWS, n_ctx), BF16),
        ],
        compiler_params=pltpu.CompilerParams(dimension_semantics=("arbitrary",), vmem_limit_bytes=VMEM_LIMIT),
        name="proj_ctx",
    )(ctx, mods, watt, gkv, wk, wvt)


def _attn_kernel(q_ref, kx_ref, kc_ref, vtx_ref, vtc_ref, o_ref, s_ref):
    heads = q_ref.shape[1]
    nq = q_ref.shape[2] // Q_TILE
    nx = kx_ref.shape[2]

    def scores(h, i, slot):
        q = q_ref[0, h, i * Q_TILE:(i + 1) * Q_TILE, :]
        s_ref[slot, 0:nx] = _dot_nt(kx_ref[0, h], q)
        s_ref[slot, nx:] = _dot_nt(kc_ref[0, h], q)

    def finish(h, i, slot):
        s_t = s_ref[slot]
        m = jnp.max(s_t, axis=0, keepdims=True)
        p_t = jnp.exp2(s_t - m).astype(BF16)
        o_t = _dot(vtx_ref[0, h], p_t[0:nx]) + _dot(vtc_ref[0, h], p_t[nx:])
        o_ref[0, h, i] = o_t[0:V_DIM] * (1.0 / o_t[V_DIM:V_DIM + 1])

    def head(h, next_h):
        for i in range(nq):
            slot = i % 2
            if i + 1 < nq:
                scores(h, i + 1, 1 - slot)
            elif next_h is not None:
                scores(next_h, 0, 1 - slot)
            finish(h, i, slot)

    assert nq % 2 == 0
    scores(0, 0, 0)

    def body(h, carry):
        head(h, h + 1)
        return carry

    lax.fori_loop(0, heads - 1, body, 0)
    head(heads - 1, None)


def _attn(q, kx, kc, vtx, vtc):
    bsz, heads, n, _ = q.shape
    n_ctx = kc.shape[2]
    n_keys = n + n_ctx
    nq = n // Q_TILE
    whole = lambda b: (b, 0, 0, 0)
    return pl.pallas_call(
        _attn_kernel,
        grid=(bsz,),
        in_specs=[
            pl.BlockSpec((1, heads, n, HEAD_PAD), whole),
            pl.BlockSpec((1, heads, n, HEAD_PAD), whole),
            pl.BlockSpec((1, heads, n_ctx, HEAD_PAD), whole),
            pl.BlockSpec((1, heads, V_ROWS, n), whole),
            pl.BlockSpec((1, heads, V_ROWS, n_ctx), whole),
        ],
        out_specs=pl.BlockSpec((1, heads, nq, V_DIM, Q_TILE), lambda b: (b, 0, 0, 0, 0)),
        out_shape=jax.ShapeDtypeStruct((bsz, heads, nq, V_DIM, Q_TILE), F32),
        scratch_shapes=[pltpu.VMEM((2, n_keys, Q_TILE), F32)],
        compiler_params=pltpu.CompilerParams(
            dimension_semantics=("arbitrary",), vmem_limit_bytes=VMEM_LIMIT),
        name="attn",
    )(q, kx, kc, vtx, vtc)


def _mixer_kernel(x_ref, xp_ref, xn_ref, mod_ref, att_ref, wt_ref, cw_ref, woc_ref, wom_ref, wo_ref, ln_ref,
                  o_ref, res_ref, *, tiles_per_seq):
    t = pl.program_id(0)
    n_tiles = pl.num_programs(0) - 1

    def layer_norm():
        res = res_ref[...]
        mu = jnp.mean(res, axis=-1, keepdims=True)
        dev = res - mu
        var = jnp.mean(dev * dev, axis=-1, keepdims=True)
        ln = ln_ref[...]
        o_ref[0] = dev * lax.rsqrt(var + LN_EPS) * ln[0:1] + ln[1:2]

    @pl.when(t == 0)
    def _():
        res_ref[...] = jnp.zeros_like(res_ref)

    @pl.when(t == n_tiles)
    def _():
        layer_norm()

    @pl.when(t < n_tiles)
    def _():
        layer_norm()
        j = t % tiles_per_seq
        d = x_ref.shape[2]
        x = x_ref[0]
        rows = x.shape[0]
        mod = mod_ref[0]
        shift = mod[0:1]
        scale1 = 1.0 + mod[1:2]
        gate = mod[2:3]
        hx = (x * scale1 + shift).astype(BF16)

        def proj(h, off, width):
            return _dot_nt(h, wt_ref[off:off + width, :])

        xh = jnp.concatenate([xp_ref[0], xn_ref[0]], axis=0)
        hh = (xh * scale1 + shift).astype(BF16)
        hx_ext = jnp.concatenate([hx, hh], axis=0)
        u_ext = proj(hx_ext, O_XC, D_CONV) * proj(hx_ext, O_CC, D_CONV)
        u, uh = u_ext[:rows], u_ext[rows:]
        prev_edge = uh[SUBLANES - 1:SUBLANES] * (j > 0).astype(F32)
        next_edge = uh[SUBLANES:SUBLANES + 1] * (j < tiles_per_seq - 1).astype(F32)

        s_conv = jax.nn.sigmoid(proj(hx, O_GCONV, d))
        s_mla = jax.nn.sigmoid(proj(hx, O_GCONV + d, d))
        gm = _silu(proj(hx, O_GM, D_MLA))

        r = lax.broadcasted_iota(jnp.int32, u.shape, 0)
        u_prev = jnp.where(r == 0, prev_edge, pltpu.roll(u, 1, 0))
        u_next = jnp.where(r == rows - 1, next_edge, pltpu.roll(u, rows - 1, 0))
        cw = cw_ref[...]
        conv = u_prev * cw[0:1] + u * cw[1:2] + u_next * cw[2:3]

        zc = _silu(proj(hx, O_GC, D_CONV)) * proj(hx, O_BC, D_CONV) * conv
        yc = _dot(zc.astype(BF16), woc_ref[...])

        att = att_ref[0, :, 0].reshape(D_MLA, rows).T
        ym = _dot((gm * att).astype(BF16), wom_ref[...])

        merged = s_conv * yc + s_mla * ym
        y = _dot(merged.astype(BF16), wo_ref[...])
        res_ref[...] = DEEPNORM_ALPHA * x + gate * y


def _mixer(x, mods, att_t, wt, cw, woc, wom, wo, ln):
    bsz, n, d = x.shape
    assert MIX_TILE == Q_TILE and n % MIX_TILE == 0
    nt = n // MIX_TILE
    n_tiles = bsz * nt
    blocks8 = MIX_TILE // SUBLANES
    const = lambda t: (0, 0)
    one = pl.Buffered(1)

    def tile(t):
        t = jnp.minimum(t, n_tiles - 1)
        return t // nt, t % nt

    def x_map(t):
        b, j = tile(t)
        return b, j, 0

    def prev_map(t):
        b, j = tile(t)
        return b, jnp.maximum(j * blocks8 - 1, 0), 0

    def next_map(t):
        b, j = tile(t)
        return b, jnp.minimum((j + 1) * blocks8, n // SUBLANES - 1), 0

    def mod_map(t):
        return tile(t)[0], 0, 0

    def att_map(t):
        b, j = tile(t)
        return b, 0, j, 0, 0

    def out_map(t):
        t = jnp.maximum(t - 1, 0)
        return t // nt, t % nt, 0

    return pl.pallas_call(
        functools.partial(_mixer_kernel, tiles_per_seq=nt),
        grid=(n_tiles + 1,),
        in_specs=[
            pl.BlockSpec((1, MIX_TILE, d), x_map),
            pl.BlockSpec((1, SUBLANES, d), prev_map),
            pl.BlockSpec((1, SUBLANES, d), next_map),
            pl.BlockSpec((1, SUBLANES, d), mod_map),
            pl.BlockSpec((1, MLA_HEADS, 1, V_DIM, MIX_TILE), att_map),
            pl.BlockSpec(wt.shape, const, pipeline_mode=one),
            pl.BlockSpec(cw.shape, const, pipeline_mode=one),
            pl.BlockSpec(woc.shape, const, pipeline_mode=one),
            pl.BlockSpec(wom.shape, const, pipeline_mode=one),
            pl.BlockSpec(wo.shape, const, pipeline_mode=one),
            pl.BlockSpec(ln.shape, const, pipeline_mode=one),
        ],
        out_specs=pl.BlockSpec((1, MIX_TILE, d), out_map),
        out_shape=jax.ShapeDtypeStruct((bsz, n, d), F32),
        scratch_shapes=[pltpu.VMEM((MIX_TILE, d), F32)],
        compiler_params=pltpu.CompilerParams(dimension_semantics=("arbitrary",), vmem_limit_bytes=VMEM_LIMIT),
        name="mixer",
    )(x, x, x, mods, att_t, wt, cw, woc, wom, wo, ln)


def _rope_pair_swap():
    q = QK_ROPE // 4
    idx = np.concatenate([np.arange(q, 2 * q), np.arange(0, q), np.arange(3 * q, 4 * q), np.arange(2 * q, 3 * q)])
    sign = np.concatenate([-np.ones(q), np.ones(q), -np.ones(q), np.ones(q)]).astype(np.float32)
    return idx, sign


def _rope_tables(n, scale):
    rows = n // GRID_W
    row_pos = np.repeat(np.arange(rows), GRID_W).astype(np.float64)
    col_pos = np.tile(np.arange(GRID_W), rows).astype(np.float64)
    axis_dim = QK_ROPE // 2
    inv_freq = ROPE_THETA ** (-np.arange(0, axis_dim, 2, dtype=np.float64) / axis_dim)
    ang_r = row_pos[:, None] * inv_freq[None, :]
    ang_c = col_pos[:, None] * inv_freq[None, :]
    cos = np.concatenate([np.cos(ang_r), np.cos(ang_r), np.cos(ang_c), np.cos(ang_c)], axis=-1)
    sin = np.concatenate([np.sin(ang_r), np.sin(ang_r), np.sin(ang_c), np.sin(ang_c)], axis=-1)
    pad = np.zeros((n, HEAD_PAD - QK_NOPE - QK_ROPE))
    zero_n = np.zeros((n, QK_NOPE))
    cq = np.concatenate([np.full((n, QK_NOPE), scale), cos * scale, pad], axis=-1)
    sq = np.concatenate([zero_n, sin * scale, pad], axis=-1)
    ck = np.concatenate([zero_n, cos, pad], axis=-1)
    sk = np.concatenate([zero_n, sin, pad], axis=-1)
    return jnp.asarray(np.concatenate([cq, sq, ck, sk], axis=-1), dtype=F32)


def _place_rope_rows(w):
    d = w.shape[1]
    return jnp.concatenate([jnp.zeros((QK_NOPE, d), w.dtype), w,
                            jnp.zeros((HEAD_PAD - QK_NOPE - QK_ROPE, d), w.dtype)], axis=0)


def kernel(x, c, ctx, c_ctx, w_ada, b_ada, w_in, conv_w, q_norm_g, w_uq, kv_norm_g, w_ukv, w_out_conv,
           w_out_mla, w_o, ln_g, ln_b):
    bsz, n, d = x.shape
    assert w_ada.shape[0] == DEPTH == 1
    w_ada, b_ada, w_in, conv_w = w_ada[0], b_ada[0], w_in[0], conv_w[0]
    q_norm_g, w_uq, kv_norm_g, w_ukv = q_norm_g[0], w_uq[0], kv_norm_g[0], w_ukv[0]
    w_out_conv, w_out_mla, w_o, ln_g, ln_b = w_out_conv[0], w_out_mla[0], w_o[0], ln_g[0], ln_b[0]

    pad_rows = (-(bsz + 1)) % SUBLANES
    c_rows = jnp.concatenate([c, c_ctx[None, :], jnp.zeros((pad_rows, d), F32)], axis=0)
    mod = _adaln(c_rows, w_ada, b_ada[None, :])
    mods = mod[:bsz + 1].reshape(bsz + 1, 3, d)
    mods = jnp.concatenate([mods, jnp.zeros((bsz + 1, SUBLANES - 3, d), F32)], axis=1)

    wt = jnp.swapaxes(w_in, 0, 1).astype(BF16)

    idx, sign = _rope_pair_swap()
    w_kr = wt[O_KR:O_KR + QK_ROPE]
    w_kr_rot = w_kr[idx] * sign[:, None].astype(BF16)
    watt = jnp.concatenate([wt[O_CQ:O_KR], _place_rope_rows(w_kr), _place_rope_rows(w_kr_rot)], axis=0)

    hq = QK_NOPE + QK_ROPE
    w_uq_h = w_uq.reshape(Q_LORA, MLA_HEADS, hq)
    zq = jnp.zeros((Q_LORA, MLA_HEADS, HEAD_PAD - hq), F32)
    wq_plain = jnp.concatenate([w_uq_h, zq], axis=-1).reshape(Q_LORA, MLA_HEADS * HEAD_PAD)
    wq_rot = jnp.concatenate([jnp.zeros((Q_LORA, MLA_HEADS, QK_NOPE), F32),
                              w_uq_h[..., QK_NOPE + idx] * sign, zq], axis=-1).reshape(Q_LORA, MLA_HEADS * HEAD_PAD)
    wq = jnp.concatenate([wq_plain, wq_rot], axis=-1).astype(BF16)

    w_ukv_h = w_ukv.reshape(KV_LORA, MLA_HEADS, QK_NOPE + V_DIM)
    wk = jnp.concatenate([w_ukv_h[..., :QK_NOPE], jnp.zeros((KV_LORA, MLA_HEADS, HEAD_PAD - QK_NOPE), F32)],
                         axis=-1).reshape(KV_LORA, MLA_HEADS * HEAD_PAD).astype(BF16)
    wv = jnp.concatenate([w_ukv_h[..., QK_NOPE:], jnp.zeros((KV_LORA, MLA_HEADS, V_ROWS - V_DIM), F32)], axis=-1)
    wvt = wv.reshape(KV_LORA, MLA_HEADS * V_ROWS).T.astype(BF16)

    tab = _rope_tables(n, float((QK_NOPE + QK_ROPE) ** -0.5 * np.log2(np.e)))
    gq, gkv = q_norm_g[None, :], kv_norm_g[None, :]
    q, kx, vtx = _proj_x(x, mods, tab, watt, gq, gkv, wq, wk, wvt)
    kc, vtc = _proj_ctx(ctx, mods, watt, gkv, wk, wvt)
    att_t = _attn(q, kx, kc, vtx, vtc)

    cw = jnp.concatenate([conv_w, jnp.zeros((SUBLANES - conv_w.shape[0], D_CONV), F32)], axis=0)
    ln = jnp.concatenate([ln_g[None, :], ln_b[None, :], jnp.zeros((SUBLANES - 2, d), F32)], axis=0)
    return _mixer(x, mods, att_t, wt, cw, w_out_conv.astype(BF16), w_out_mla.astype(BF16), w_o.astype(BF16), ln)
```

```python
import functools

import jax
import jax.numpy as jnp
import numpy as np
from jax import lax
from jax.experimental import pallas as pl
from jax.experimental.pallas import tpu as pltpu

GRID_W = 64
D_CONV = 512
MLA_HEADS = 8
QK_NOPE = 64
QK_ROPE = 32
V_DIM = 64
D_MLA = MLA_HEADS * V_DIM
Q_LORA = 256
KV_LORA = 128
ROPE_THETA = 10000.0
LN_EPS = 1e-5
RMS_EPS = 1e-6
DEPTH = 1
DEEPNORM_ALPHA = (2.0 * DEPTH) ** 0.25

O_XC, O_BC, O_CC, O_GC = 0, D_CONV, 2 * D_CONV, 3 * D_CONV
O_CQ = 4 * D_CONV
O_CKV = O_CQ + Q_LORA
O_KR = O_CKV + KV_LORA
O_GM = O_KR + QK_ROPE
O_GCONV = O_GM + D_MLA

LANES = 128
SUBLANES = 8
HEAD_PAD = LANES
V_ROWS = V_DIM + 16
PROJ_TILE = 512
MIX_TILE = 512
Q_TILE = 512
VMEM_LIMIT = 56 * 1024 * 1024

F32 = jnp.float32
BF16 = jnp.bfloat16


def _dot(a, b):
    return jnp.dot(a, b, preferred_element_type=F32)


def _dot_nt(a, b):
    return lax.dot_general(a, b, (((1,), (1,)), ((), ())), preferred_element_type=F32)


def _silu(t):
    return t * jax.nn.sigmoid(t)


def _rms_norm(t, g):
    return t * lax.rsqrt(jnp.mean(t * t, axis=-1, keepdims=True) + RMS_EPS) * g


def _adaln_kernel(c_ref, w_ref, b_ref, o_ref):
    cs = _silu(c_ref[...])
    o_ref[...] = _dot(cs, w_ref[...]) + b_ref[...]


def _adaln(c_rows, w_ada, b_ada):
    rows, d = c_rows.shape
    n_out = w_ada.shape[1]
    col_tile = 512
    return pl.pallas_call(
        _adaln_kernel,
        grid=(n_out // col_tile,),
        in_specs=[
            pl.BlockSpec((rows, d), lambda i: (0, 0)),
            pl.BlockSpec((d, col_tile), lambda i: (0, i)),
            pl.BlockSpec((1, col_tile), lambda i: (0, i)),
        ],
        out_specs=pl.BlockSpec((rows, col_tile), lambda i: (0, i)),
        out_shape=jax.ShapeDtypeStruct((rows, n_out), F32),
        name="adaln",
    )(c_rows, w_ada, b_ada)


def _keys_values(ckv, kr_placed, gkv_ref, wk_ref, wvt_ref, k_ref, vt_ref):
    kvn = _rms_norm(ckv, gkv_ref[...]).astype(BF16)
    kn = _dot(kvn, wk_ref[...])
    for h in range(MLA_HEADS):
        k_ref[0, h] = (kn[:, h * HEAD_PAD:(h + 1) * HEAD_PAD] + kr_placed).astype(BF16)
    vt = _dot_nt(wvt_ref[...], kvn)
    r = lax.broadcasted_iota(jnp.int32, vt.shape, 0)
    ones = r == V_DIM
    for h in range(1, MLA_HEADS):
        ones = ones | (r == h * V_ROWS + V_DIM)
    vt = jnp.where(ones, 1.0, vt)
    vt_ref[0] = vt.astype(BF16).reshape(MLA_HEADS, V_ROWS, vt.shape[1])


def _proj_x_kernel(x_ref, mod_ref, tab_ref, watt_ref, gq_ref, gkv_ref, wq_ref, wk_ref, wvt_ref,
                   q_ref, k_ref, vt_ref):
    mod = mod_ref[0]
    h = (x_ref[0] * (1.0 + mod[1:2]) + mod[0:1]).astype(BF16)
    p = _dot_nt(h, watt_ref[...])
    tab = tab_ref[...]
    cq_t, sq_t = tab[:, 0:LANES], tab[:, LANES:2 * LANES]
    ck_t, sk_t = tab[:, 2 * LANES:3 * LANES], tab[:, 3 * LANES:4 * LANES]
    kr0 = Q_LORA + KV_LORA
    kr_placed = p[:, kr0:kr0 + LANES] * ck_t + p[:, kr0 + LANES:kr0 + 2 * LANES] * sk_t
    _keys_values(p[:, Q_LORA:kr0], kr_placed, gkv_ref, wk_ref, wvt_ref, k_ref, vt_ref)
    qn = _rms_norm(p[:, 0:Q_LORA], gq_ref[...]).astype(BF16)
    qq = _dot(qn, wq_ref[...])
    width = MLA_HEADS * HEAD_PAD
    for hd in range(MLA_HEADS):
        a = qq[:, hd * HEAD_PAD:(hd + 1) * HEAD_PAD]
        b = qq[:, width + hd * HEAD_PAD:width + (hd + 1) * HEAD_PAD]
        q_ref[0, hd] = (a * cq_t + b * sq_t).astype(BF16)


def _proj_ctx_kernel(ctx_ref, mod_ref, watt_ref, gkv_ref, wk_ref, wvt_ref, k_ref, vt_ref):
    mod = mod_ref[0]
    h = (ctx_ref[0] * (1.0 + mod[1:2]) + mod[0:1]).astype(BF16)
    p = _dot_nt(h, watt_ref[Q_LORA:Q_LORA + KV_LORA + LANES, :])
    _keys_values(p[:, 0:KV_LORA], p[:, KV_LORA:], gkv_ref, wk_ref, wvt_ref, k_ref, vt_ref)


def _proj_x(x, mods, tab, watt, gq, gkv, wq, wk, wvt):
    bsz, n, d = x.shape
    assert n % PROJ_TILE == 0
    const = lambda j, b: (0, 0)
    one = pl.Buffered(1)
    head_block = pl.BlockSpec((1, MLA_HEADS, PROJ_TILE, HEAD_PAD), lambda j, b: (b, 0, j, 0))
    return pl.pallas_call(
        _proj_x_kernel,
        grid=(n // PROJ_TILE, bsz),
        in_specs=[
            pl.BlockSpec((1, PROJ_TILE, d), lambda j, b: (b, j, 0)),
            pl.BlockSpec((1, SUBLANES, d), lambda j, b: (b, 0, 0)),
            pl.BlockSpec((PROJ_TILE, 4 * LANES), lambda j, b: (j, 0)),
            pl.BlockSpec(watt.shape, const, pipeline_mode=one),
            pl.BlockSpec(gq.shape, const, pipeline_mode=one),
            pl.BlockSpec(gkv.shape, const, pipeline_mode=one),
            pl.BlockSpec(wq.shape, const, pipeline_mode=one),
            pl.BlockSpec(wk.shape, const, pipeline_mode=one),
            pl.BlockSpec(wvt.shape, const, pipeline_mode=one),
        ],
        out_specs=[head_block, head_block,
                   pl.BlockSpec((1, MLA_HEADS, V_ROWS, PROJ_TILE), lambda j, b: (b, 0, 0, j))],
        out_shape=[
            jax.ShapeDtypeStruct((bsz, MLA_HEADS, n, HEAD_PAD), BF16),
            jax.ShapeDtypeStruct((bsz, MLA_HEADS, n, HEAD_PAD), BF16),
            jax.ShapeDtypeStruct((bsz, MLA_HEADS, V_ROWS, n), BF16),
        ],
        compiler_params=pltpu.CompilerParams(
            dimension_semantics=("arbitrary", "arbitrary"), vmem_limit_bytes=VMEM_LIMIT),
        name="proj_x",
    )(x, mods, tab, watt, gq, gkv, wq, wk, wvt)


def _proj_ctx(ctx, mods, watt, gkv, wk, wvt):
    bsz, n_ctx, d = ctx.shape
    const = lambda b: (0, 0)
    one = pl.Buffered(1)
    return pl.pallas_call(
        _proj_ctx_kernel,
        grid=(bsz,),
        in_specs=[
            pl.BlockSpec((1, n_ctx, d), lambda b: (b, 0, 0)),
            pl.BlockSpec((1, SUBLANES, d), lambda b: (bsz, 0, 0)),
            pl.BlockSpec(watt.shape, const, pipeline_mode=one),
            pl.BlockSpec(gkv.shape, const, pipeline_mode=one),
            pl.BlockSpec(wk.shape, const, pipeline_mode=one),
            pl.BlockSpec(wvt.shape, const, pipeline_mode=one),
        ],
        out_specs=[pl.BlockSpec((1, MLA_HEADS, n_ctx, HEAD_PAD), lambda b: (b, 0, 0, 0)),
                   pl.BlockSpec((1, MLA_HEADS, V_ROWS, n_ctx), lambda b: (b, 0, 0, 0))],
        out_shape=[
            jax.ShapeDtypeStruct((bsz, MLA_HEADS, n_ctx, HEAD_PAD), BF16),
            jax.ShapeDtypeStruct((bsz, MLA_HEADS, V_ROWS, n_ctx), BF16),
        ],
        compiler_params=pltpu.CompilerParams(dimension_semantics=("arbitrary",), vmem_limit_bytes=VMEM_LIMIT),
        name="proj_ctx",
    )(ctx, mods, watt, gkv, wk, wvt)


def _attn_kernel(q_ref, kx_ref, kc_ref, vtx_ref, vtc_ref, o_ref, s_ref):
    heads = q_ref.shape[1]
    nq = q_ref.shape[2] // Q_TILE
    nx = kx_ref.shape[2]

    def scores(h, i, slot):
        q = q_ref[0, h, i * Q_TILE:(i + 1) * Q_TILE, :]
        s_ref[slot, 0:nx] = _dot_nt(kx_ref[0, h], q)
        s_ref[slot, nx:] = _dot_nt(kc_ref[0, h], q)

    def finish(h, i, slot):
        s_t = s_ref[slot]
        m = jnp.max(s_t, axis=0, keepdims=True)
        p_t = jnp.exp2(s_t - m).astype(BF16)
        o_t = _dot(vtx_ref[0, h], p_t[0:nx]) + _dot(vtc_ref[0, h], p_t[nx:])
        o_ref[0, h, i] = o_t[0:V_DIM] * (1.0 / o_t[V_DIM:V_DIM + 1])

    def head(h, next_h):
        for i in range(nq):
            slot = i % 2
            if i + 1 < nq:
                scores(h, i + 1, 1 - slot)
            elif next_h is not None:
                scores(next_h, 0, 1 - slot)
            finish(h, i, slot)

    assert nq % 2 == 0
    scores(0, 0, 0)

    def body(h, carry):
        head(h, h + 1)
        return carry

    lax.fori_loop(0, heads - 1, body, 0)
    head(heads - 1, None)


def _attn(q, kx, kc, vtx, vtc):
    bsz, heads, n, _ = q.shape
    n_ctx = kc.shape[2]
    n_keys = n + n_ctx
    nq = n // Q_TILE
    whole = lambda b: (b, 0, 0, 0)
    return pl.pallas_call(
        _attn_kernel,
        grid=(bsz,),
        in_specs=[
            pl.BlockSpec((1, heads, n, HEAD_PAD), whole),
            pl.BlockSpec((1, heads, n, HEAD_PAD), whole),
            pl.BlockSpec((1, heads, n_ctx, HEAD_PAD), whole),
            pl.BlockSpec((1, heads, V_ROWS, n), whole),
            pl.BlockSpec((1, heads, V_ROWS, n_ctx), whole),
        ],
        out_specs=pl.BlockSpec((1, heads, nq, V_DIM, Q_TILE), lambda b: (b, 0, 0, 0, 0)),
        out_shape=jax.ShapeDtypeStruct((bsz, heads, nq, V_DIM, Q_TILE), F32),
        scratch_shapes=[pltpu.VMEM((2, n_keys, Q_TILE), F32)],
        compiler_params=pltpu.CompilerParams(
            dimension_semantics=("arbitrary",), vmem_limit_bytes=VMEM_LIMIT),
        name="attn",
    )(q, kx, kc, vtx, vtc)


def _mixer_kernel(x_ref, xp_ref, xn_ref, mod_ref, att_ref, wt_ref, cw_ref, woc_ref, wom_ref, wo_ref, ln_ref,
                  o_ref, res_ref, *, tiles_per_seq):
    t = pl.program_id(0)
    n_tiles = pl.num_programs(0) - 1

    def layer_norm():
        res = res_ref[...]
        mu = jnp.mean(res, axis=-1, keepdims=True)
        dev = res - mu
        var = jnp.mean(dev * dev, axis=-1, keepdims=True)
        ln = ln_ref[...]
        o_ref[0] = dev * lax.rsqrt(var + LN_EPS) * ln[0:1] + ln[1:2]

    @pl.when(t == 0)
    def _():
        res_ref[...] = jnp.zeros_like(res_ref)

    @pl.when(t == n_tiles)
    def _():
        layer_norm()

    @pl.when(t < n_tiles)
    def _():
        layer_norm()
        j = t % tiles_per_seq
        d = x_ref.shape[2]
        x = x_ref[0]
        rows = x.shape[0]
        mod = mod_ref[0]
        shift = mod[0:1]
        scale1 = 1.0 + mod[1:2]
        gate = mod[2:3]
        hx = (x * scale1 + shift).astype(BF16)

        def proj(h, off, width):
            return _dot_nt(h, wt_ref[off:off + width, :])

        xh = jnp.concatenate([xp_ref[0], xn_ref[0]], axis=0)
        hh = (xh * scale1 + shift).astype(BF16)
        hx_ext = jnp.concatenate([hx, hh], axis=0)
        u_ext = proj(hx_ext, O_XC, D_CONV) * proj(hx_ext, O_CC, D_CONV)
        u, uh = u_ext[:rows], u_ext[rows:]
        prev_edge = uh[SUBLANES - 1:SUBLANES] * (j > 0).astype(F32)
        next_edge = uh[SUBLANES:SUBLANES + 1] * (j < tiles_per_seq - 1).astype(F32)

        s_conv = jax.nn.sigmoid(proj(hx, O_GCONV, d))
        s_mla = jax.nn.sigmoid(proj(hx, O_GCONV + d, d))
        gm = _silu(proj(hx, O_GM, D_MLA))

        r = lax.broadcasted_iota(jnp.int32, u.shape, 0)
        u_prev = jnp.where(r == 0, prev_edge, pltpu.roll(u, 1, 0))
        u_next = jnp.where(r == rows - 1, next_edge, pltpu.roll(u, rows - 1, 0))
        cw = cw_ref[...]
        conv = u_prev * cw[0:1] + u * cw[1:2] + u_next * cw[2:3]

        zc = _silu(proj(hx, O_GC, D_CONV)) * proj(hx, O_BC, D_CONV) * conv
        yc = _dot(zc.astype(BF16), woc_ref[...])

        att = att_ref[0, :, 0].reshape(D_MLA, rows).T
        ym = _dot((gm * att).astype(BF16), wom_ref[...])

        merged = s_conv * yc + s_mla * ym
        y = _dot(merged.astype(BF16), wo_ref[...])
        res_ref[...] = DEEPNORM_ALPHA * x + gate * y


def _mixer(x, mods, att_t, wt, cw, woc, wom, wo, ln):
    bsz, n, d = x.shape
    assert MIX_TILE == Q_TILE and n % MIX_TILE == 0
    nt = n // MIX_TILE
    n_tiles = bsz * nt
    blocks8 = MIX_TILE // SUBLANES
    const = lambda t: (0, 0)
    one = pl.Buffered(1)

    def tile(t):
        t = jnp.minimum(t, n_tiles - 1)
        return t // nt, t % nt

    def x_map(t):
        b, j = tile(t)
        return b, j, 0

    def prev_map(t):
        b, j = tile(t)
        return b, jnp.maximum(j * blocks8 - 1, 0), 0

    def next_map(t):
        b, j = tile(t)
        return b, jnp.minimum((j + 1) * blocks8, n // SUBLANES - 1), 0

    def mod_map(t):
        return tile(t)[0], 0, 0

    def att_map(t):
        b, j = tile(t)
        return b, 0, j, 0, 0

    def out_map(t):
        t = jnp.maximum(t - 1, 0)
        return t // nt, t % nt, 0

    return pl.pallas_call(
        functools.partial(_mixer_kernel, tiles_per_seq=nt),
        grid=(n_tiles + 1,),
        in_specs=[
            pl.BlockSpec((1, MIX_TILE, d), x_map),
            pl.BlockSpec((1, SUBLANES, d), prev_map),
            pl.BlockSpec((1, SUBLANES, d), next_map),
            pl.BlockSpec((1, SUBLANES, d), mod_map),
            pl.BlockSpec((1, MLA_HEADS, 1, V_DIM, MIX_TILE), att_map),
            pl.BlockSpec(wt.shape, const, pipeline_mode=one),
            pl.BlockSpec(cw.shape, const, pipeline_mode=one),
            pl.BlockSpec(woc.shape, const, pipeline_mode=one),
            pl.BlockSpec(wom.shape, const, pipeline_mode=one),
            pl.BlockSpec(wo.shape, const, pipeline_mode=one),
            pl.BlockSpec(ln.shape, const, pipeline_mode=one),
        ],
        out_specs=pl.BlockSpec((1, MIX_TILE, d), out_map),
        out_shape=jax.ShapeDtypeStruct((bsz, n, d), F32),
        scratch_shapes=[pltpu.VMEM((MIX_TILE, d), F32)],
        compiler_params=pltpu.CompilerParams(dimension_semantics=("arbitrary",), vmem_limit_bytes=VMEM_LIMIT),
        name="mixer",
    )(x, x, x, mods, att_t, wt, cw, woc, wom, wo, ln)


def _rope_pair_swap():
    q = QK_ROPE // 4
    idx = np.concatenate([np.arange(q, 2 * q), np.arange(0, q), np.arange(3 * q, 4 * q), np.arange(2 * q, 3 * q)])
    sign = np.concatenate([-np.ones(q), np.ones(q), -np.ones(q), np.ones(q)]).astype(np.float32)
    return idx, sign


def _rope_tables(n, scale):
    rows = n // GRID_W
    row_pos = np.repeat(np.arange(rows), GRID_W).astype(np.float64)
    col_pos = np.tile(np.arange(GRID_W), rows).astype(np.float64)
    axis_dim = QK_ROPE // 2
    inv_freq = ROPE_THETA ** (-np.arange(0, axis_dim, 2, dtype=np.float64) / axis_dim)
    ang_r = row_pos[:, None] * inv_freq[None, :]
    ang_c = col_pos[:, None] * inv_freq[None, :]
    cos = np.concatenate([np.cos(ang_r), np.cos(ang_r), np.cos(ang_c), np.cos(ang_c)], axis=-1)
    sin = np.concatenate([np.sin(ang_r), np.sin(ang_r), np.sin(ang_c), np.sin(ang_c)], axis=-1)
    pad = np.zeros((n, HEAD_PAD - QK_NOPE - QK_ROPE))
    zero_n = np.zeros((n, QK_NOPE))
    cq = np.concatenate([np.full((n, QK_NOPE), scale), cos * scale, pad], axis=-1)
    sq = np.concatenate([zero_n, sin * scale, pad], axis=-1)
    ck = np.concatenate([zero_n, cos, pad], axis=-1)
    sk = np.concatenate([zero_n, sin, pad], axis=-1)
    return jnp.asarray(np.concatenate([cq, sq, ck, sk], axis=-1), dtype=F32)


def _place_rope_rows(w):
    d = w.shape[1]
    return jnp.concatenate([jnp.zeros((QK_NOPE, d), w.dtype), w,
                            jnp.zeros((HEAD_PAD - QK_NOPE - QK_ROPE, d), w.dtype)], axis=0)


def kernel(x, c, ctx, c_ctx, w_ada, b_ada, w_in, conv_w, q_norm_g, w_uq, kv_norm_g, w_ukv, w_out_conv,
           w_out_mla, w_o, ln_g, ln_b):
    bsz, n, d = x.shape
    assert w_ada.shape[0] == DEPTH == 1
    w_ada, b_ada, w_in, conv_w = w_ada[0], b_ada[0], w_in[0], conv_w[0]
    q_norm_g, w_uq, kv_norm_g, w_ukv = q_norm_g[0], w_uq[0], kv_norm_g[0], w_ukv[0]
    w_out_conv, w_out_mla, w_o, ln_g, ln_b = w_out_conv[0], w_out_mla[0], w_o[0], ln_g[0], ln_b[0]

    pad_rows = (-(bsz + 1)) % SUBLANES
    c_rows = jnp.concatenate([c, c_ctx[None, :], jnp.zeros((pad_rows, d), F32)], axis=0)
    mod = _adaln(c_rows, w_ada, b_ada[None, :])
    mods = mod[:bsz + 1].reshape(bsz + 1, 3, d)
    mods = jnp.concatenate([mods, jnp.zeros((bsz + 1, SUBLANES - 3, d), F32)], axis=1)

    wt = jnp.swapaxes(w_in, 0, 1).astype(BF16)

    idx, sign = _rope_pair_swap()
    w_kr = wt[O_KR:O_KR + QK_ROPE]
    w_kr_rot = w_kr[idx] * sign[:, None].astype(BF16)
    watt = jnp.concatenate([wt[O_CQ:O_KR], _place_rope_rows(w_kr), _place_rope_rows(w_kr_rot)], axis=0)

    hq = QK_NOPE + QK_ROPE
    w_uq_h = w_uq.reshape(Q_LORA, MLA_HEADS, hq)
    zq = jnp.zeros((Q_LORA, MLA_HEADS, HEAD_PAD - hq), F32)
    wq_plain = jnp.concatenate([w_uq_h, zq], axis=-1).reshape(Q_LORA, MLA_HEADS * HEAD_PAD)
    wq_rot = jnp.concatenate([jnp.zeros((Q_LORA, MLA_HEADS, QK_NOPE), F32),
                              w_uq_h[..., QK_NOPE + idx] * sign, zq], axis=-1).reshape(Q_LORA, MLA_HEADS * HEAD_PAD)
    wq = jnp.concatenate([wq_plain, wq_rot], axis=-1).astype(BF16)

    w_ukv_h = w_ukv.reshape(KV_LORA, MLA_HEADS, QK_NOPE + V_DIM)
    wk = jnp.concatenate([w_ukv_h[..., :QK_NOPE], jnp.zeros((KV_LORA, MLA_HEADS, HEAD_PAD - QK_NOPE), F32)],
                         axis=-1).reshape(KV_LORA, MLA_HEADS * HEAD_PAD).astype(BF16)
    wv = jnp.concatenate([w_ukv_h[..., QK_NOPE:], jnp.zeros((KV_LORA, MLA_HEADS, V_ROWS - V_DIM), F32)], axis=-1)
    wvt = wv.reshape(KV_LORA, MLA_HEADS * V_ROWS).T.astype(BF16)

    tab = _rope_tables(n, float((QK_NOPE + QK_ROPE) ** -0.5 * np.log2(np.e)))
    gq, gkv = q_norm_g[None, :], kv_norm_g[None, :]
    q, kx, vtx = _proj_x(x, mods, tab, watt, gq, gkv, wq, wk, wvt)
    kc, vtc = _proj_ctx(ctx, mods, watt, gkv, wk, wvt)
    att_t = _attn(q, kx, kc, vtx, vtc)

    cw = jnp.concatenate([conv_w, jnp.zeros((SUBLANES - conv_w.shape[0], D_CONV), F32)], axis=0)
    ln = jnp.concatenate([ln_g[None, :], ln_b[None, :], jnp.zeros((SUBLANES - 2, d), F32)], axis=0)
    return _mixer(x, mods, att_t, wt, cw, w_out_conv.astype(BF16), w_out_mla.astype(BF16), w_o.astype(BF16), ln)
```

```python
import functools

import jax
import jax.numpy as jnp
import numpy as np
from jax import lax
from jax.experimental import pallas as pl
from jax.experimental.pallas import tpu as pltpu

GRID_W = 64
D_CONV = 512
MLA_HEADS = 8
QK_NOPE = 64
QK_ROPE = 32
V_DIM = 64
D_MLA = MLA_HEADS * V_DIM
Q_LORA = 256
KV_LORA = 128
ROPE_THETA = 10000.0
LN_EPS = 1e-5
RMS_EPS = 1e-6
DEPTH = 1
DEEPNORM_ALPHA = (2.0 * DEPTH) ** 0.25

O_XC, O_BC, O_CC, O_GC = 0, D_CONV, 2 * D_CONV, 3 * D_CONV
O_CQ = 4 * D_CONV
O_CKV = O_CQ + Q_LORA
O_KR = O_CKV + KV_LORA
O_GM = O_KR + QK_ROPE
O_GCONV = O_GM + D_MLA

LANES = 128
SUBLANES = 8
HEAD_PAD = LANES
V_ROWS = V_DIM + 16
PROJ_TILE = 512
MIX_TILE = 512
Q_TILE = 512
VMEM_LIMIT = 56 * 1024 * 1024

F32 = jnp.float32
BF16 = jnp.bfloat16


def _dot(a, b):
    return jnp.dot(a, b, preferred_element_type=F32)


def _dot_nt(a, b):
    return lax.dot_general(a, b, (((1,), (1,)), ((), ())), preferred_element_type=F32)


def _silu(t):
    return t * jax.nn.sigmoid(t)


def _rms_norm(t, g):
    return t * lax.rsqrt(jnp.mean(t * t, axis=-1, keepdims=True) + RMS_EPS) * g


def _adaln_kernel(c_ref, w_ref, b_ref, o_ref):
    cs = _silu(c_ref[...])
    o_ref[...] = _dot(cs, w_ref[...]) + b_ref[...]


def _adaln(c_rows, w_ada, b_ada):
    rows, d = c_rows.shape
    n_out = w_ada.shape[1]
    col_tile = 512
    return pl.pallas_call(
        _adaln_kernel,
        grid=(n_out // col_tile,),
        in_specs=[
            pl.BlockSpec((rows, d), lambda i: (0, 0)),
            pl.BlockSpec((d, col_tile), lambda i: (0, i)),
            pl.BlockSpec((1, col_tile), lambda i: (0, i)),
        ],
        out_specs=pl.BlockSpec((rows, col_tile), lambda i: (0, i)),
        out_shape=jax.ShapeDtypeStruct((rows, n_out), F32),
        name="adaln",
    )(c_rows, w_ada, b_ada)


def _keys_values(ckv, kr_placed, gkv_ref, wk_ref, wvt_ref, k_ref, vt_ref):
    kvn = _rms_norm(ckv, gkv_ref[...])
    kn = _dot(kvn, wk_ref[...])
    for h in range(MLA_HEADS):
        k_ref[0, h] = (kn[:, h * HEAD_PAD:(h + 1) * HEAD_PAD] + kr_placed).astype(BF16)
    vt = _dot_nt(wvt_ref[...], kvn)
    r = lax.broadcasted_iota(jnp.int32, vt.shape, 0)
    ones = r == V_DIM
    for h in range(1, MLA_HEADS):
        ones = ones | (r == h * V_ROWS + V_DIM)
    vt = jnp.where(ones, 1.0, vt)
    vt_ref[0] = vt.astype(BF16).reshape(MLA_HEADS, V_ROWS, vt.shape[1])


def _proj_x_kernel(x_ref, mod_ref, tab_ref, watt_ref, gq_ref, gkv_ref, wq_ref, wk_ref, wvt_ref,
                   q_ref, k_ref, vt_ref):
    mod = mod_ref[0]
    h = x_ref[0] * (1.0 + mod[1:2]) + mod[0:1]
    p = _dot_nt(h, watt_ref[...])
    tab = tab_ref[...]
    cq_t, sq_t = tab[:, 0:LANES], tab[:, LANES:2 * LANES]
    ck_t, sk_t = tab[:, 2 * LANES:3 * LANES], tab[:, 3 * LANES:4 * LANES]
    kr0 = Q_LORA + KV_LORA
    kr_placed = p[:, kr0:kr0 + LANES] * ck_t + p[:, kr0 + LANES:kr0 + 2 * LANES] * sk_t
    _keys_values(p[:, Q_LORA:kr0], kr_placed, gkv_ref, wk_ref, wvt_ref, k_ref, vt_ref)
    qn = _rms_norm(p[:, 0:Q_LORA], gq_ref[...])
    qq = _dot(qn, wq_ref[...])
    width = MLA_HEADS * HEAD_PAD
    for hd in range(MLA_HEADS):
        a = qq[:, hd * HEAD_PAD:(hd + 1) * HEAD_PAD]
        b = qq[:, width + hd * HEAD_PAD:width + (hd + 1) * HEAD_PAD]
        q_ref[0, hd] = (a * cq_t + b * sq_t).astype(BF16)


def _proj_ctx_kernel(ctx_ref, mod_ref, watt_ref, gkv_ref, wk_ref, wvt_ref, k_ref, vt_ref):
    mod = mod_ref[0]
    h = ctx_ref[0] * (1.0 + mod[1:2]) + mod[0:1]
    p = _dot_nt(h, watt_ref[Q_LORA:Q_LORA + KV_LORA + LANES, :])
    _keys_values(p[:, 0:KV_LORA], p[:, KV_LORA:], gkv_ref, wk_ref, wvt_ref, k_ref, vt_ref)


def _proj_x(x, mods, tab, watt, gq, gkv, wq, wk, wvt):
    bsz, n, d = x.shape
    assert n % PROJ_TILE == 0
    const = lambda j, b: (0, 0)
    one = pl.Buffered(1)
    head_block = pl.BlockSpec((1, MLA_HEADS, PROJ_TILE, HEAD_PAD), lambda j, b: (b, 0, j, 0))
    return pl.pallas_call(
        _proj_x_kernel,
        grid=(n // PROJ_TILE, bsz),
        in_specs=[
            pl.BlockSpec((1, PROJ_TILE, d), lambda j, b: (b, j, 0)),
            pl.BlockSpec((1, SUBLANES, d), lambda j, b: (b, 0, 0)),
            pl.BlockSpec((PROJ_TILE, 4 * LANES), lambda j, b: (j, 0)),
            pl.BlockSpec(watt.shape, const, pipeline_mode=one),
            pl.BlockSpec(gq.shape, const, pipeline_mode=one),
            pl.BlockSpec(gkv.shape, const, pipeline_mode=one),
            pl.BlockSpec(wq.shape, const, pipeline_mode=one),
            pl.BlockSpec(wk.shape, const, pipeline_mode=one),
            pl.BlockSpec(wvt.shape, const, pipeline_mode=one),
        ],
        out_specs=[head_block, head_block,
                   pl.BlockSpec((1, MLA_HEADS, V_ROWS, PROJ_TILE), lambda j, b: (b, 0, 0, j))],
        out_shape=[
            jax.ShapeDtypeStruct((bsz, MLA_HEADS, n, HEAD_PAD), BF16),
            jax.ShapeDtypeStruct((bsz, MLA_HEADS, n, HEAD_PAD), BF16),
            jax.ShapeDtypeStruct((bsz, MLA_HEADS, V_ROWS, n), BF16),
        ],
        compiler_params=pltpu.CompilerParams(
            dimension_semantics=("arbitrary", "arbitrary"), vmem_limit_bytes=VMEM_LIMIT),
        name="proj_x",
    )(x, mods, tab, watt, gq, gkv, wq, wk, wvt)


def _proj_ctx(ctx, mods, watt, gkv, wk, wvt):
    bsz, n_ctx, d = ctx.shape
    const = lambda b: (0, 0)
    one = pl.Buffered(1)
    return pl.pallas_call(
        _proj_ctx_kernel,
        grid=(bsz,),
        in_specs=[
            pl.BlockSpec((1, n_ctx, d), lambda b: (b, 0, 0)),
            pl.BlockSpec((1, SUBLANES, d), lambda b: (bsz, 0, 0)),
            pl.BlockSpec(watt.shape, const, pipeline_mode=one),
            pl.BlockSpec(gkv.shape, const, pipeline_mode=one),
            pl.BlockSpec(wk.shape, const, pipeline_mode=one),
            pl.BlockSpec(wvt.shape, const, pipeline_mode=one),
        ],
        out_specs=[pl.BlockSpec((1, MLA_HEADS, n_ctx, HEAD_PAD), lambda b: (b, 0, 0, 0)),
                   pl.BlockSpec((1, MLA_HEADS, V_ROWS, n_ctx), lambda b: (b, 0, 0, 0))],
        out_shape=[
            jax.ShapeDtypeStruct((bsz, MLA_HEADS, n_ctx, HEAD_PAD), BF16),
            jax.ShapeDtypeStruct((bsz, MLA_HEADS, V_ROWS, n_ctx), BF16),
        ],
        compiler_params=pltpu.CompilerParams(dimension_semantics=("arbitrary",), vmem_limit_bytes=VMEM_LIMIT),
        name="proj_ctx",
    )(ctx, mods, watt, gkv, wk, wvt)


def _attn_kernel(q_ref, kx_ref, kc_ref, vtx_ref, vtc_ref, o_ref, s_ref):
    heads = q_ref.shape[1]
    nq = q_ref.shape[2] // Q_TILE
    nx = kx_ref.shape[2]

    def scores(h, i, slot):
        q = q_ref[0, h, i * Q_TILE:(i + 1) * Q_TILE, :]
        s_ref[slot, 0:nx] = _dot_nt(kx_ref[0, h], q)
        s_ref[slot, nx:] = _dot_nt(kc_ref[0, h], q)

    def finish(h, i, slot):
        s_t = s_ref[slot]
        m = jnp.max(s_t, axis=0, keepdims=True)
        p_t = jnp.exp2(s_t - m).astype(BF16)
        o_t = _dot(vtx_ref[0, h], p_t[0:nx]) + _dot(vtc_ref[0, h], p_t[nx:])
        o_ref[0, h, i] = o_t[0:V_DIM] * (1.0 / o_t[V_DIM:V_DIM + 1])

    def head(h, next_h):
        for i in range(nq):
            slot = i % 2
            if i + 1 < nq:
                scores(h, i + 1, 1 - slot)
            elif next_h is not None:
                scores(next_h, 0, 1 - slot)
            finish(h, i, slot)

    assert nq % 2 == 0
    scores(0, 0, 0)

    def body(h, carry):
        head(h, h + 1)
        return carry

    lax.fori_loop(0, heads - 1, body, 0)
    head(heads - 1, None)


def _attn(q, kx, kc, vtx, vtc):
    bsz, heads, n, _ = q.shape
    n_ctx = kc.shape[2]
    n_keys = n + n_ctx
    nq = n // Q_TILE
    whole = lambda b: (b, 0, 0, 0)
    return pl.pallas_call(
        _attn_kernel,
        grid=(bsz,),
        in_specs=[
            pl.BlockSpec((1, heads, n, HEAD_PAD), whole),
            pl.BlockSpec((1, heads, n, HEAD_PAD), whole),
            pl.BlockSpec((1, heads, n_ctx, HEAD_PAD), whole),
            pl.BlockSpec((1, heads, V_ROWS, n), whole),
            pl.BlockSpec((1, heads, V_ROWS, n_ctx), whole),
        ],
        out_specs=pl.BlockSpec((1, heads, nq, V_DIM, Q_TILE), lambda b: (b, 0, 0, 0, 0)),
        out_shape=jax.ShapeDtypeStruct((bsz, heads, nq, V_DIM, Q_TILE), F32),
        scratch_shapes=[pltpu.VMEM((2, n_keys, Q_TILE), F32)],
        compiler_params=pltpu.CompilerParams(
            dimension_semantics=("arbitrary",), vmem_limit_bytes=VMEM_LIMIT),
        name="attn",
    )(q, kx, kc, vtx, vtc)


def _mixer_kernel(x_ref, xp_ref, xn_ref, mod_ref, att_ref, wt_ref, cw_ref, woc_ref, wom_ref, wo_ref, ln_ref,
                  o_ref, res_ref, *, tiles_per_seq):
    t = pl.program_id(0)
    n_tiles = pl.num_programs(0) - 1

    def layer_norm():
        res = res_ref[...]
        mu = jnp.mean(res, axis=-1, keepdims=True)
        dev = res - mu
        var = jnp.mean(dev * dev, axis=-1, keepdims=True)
        ln = ln_ref[...]
        o_ref[0] = dev * lax.rsqrt(var + LN_EPS) * ln[0:1] + ln[1:2]

    @pl.when(t == 0)
    def _():
        res_ref[...] = jnp.zeros_like(res_ref)

    @pl.when(t == n_tiles)
    def _():
        layer_norm()

    @pl.when(t < n_tiles)
    def _():
        layer_norm()
        j = t % tiles_per_seq
        d = x_ref.shape[2]
        x = x_ref[0]
        rows = x.shape[0]
        mod = mod_ref[0]
        shift = mod[0:1]
        scale1 = 1.0 + mod[1:2]
        gate = mod[2:3]
        hx = x * scale1 + shift

        def proj(h, off, width):
            return _dot_nt(h, wt_ref[off:off + width, :])

        xh = jnp.concatenate([xp_ref[0], xn_ref[0]], axis=0)
        hh = xh * scale1 + shift
        hx_ext = jnp.concatenate([hx, hh], axis=0)
        u_ext = proj(hx_ext, O_XC, D_CONV) * proj(hx_ext, O_CC, D_CONV)
        u, uh = u_ext[:rows], u_ext[rows:]
        prev_edge = uh[SUBLANES - 1:SUBLANES] * (j > 0).astype(F32)
        next_edge = uh[SUBLANES:SUBLANES + 1] * (j < tiles_per_seq - 1).astype(F32)

        s_conv = jax.nn.sigmoid(proj(hx, O_GCONV, d))
        s_mla = jax.nn.sigmoid(proj(hx, O_GCONV + d, d))
        gm = _silu(proj(hx, O_GM, D_MLA))

        r = lax.broadcasted_iota(jnp.int32, u.shape, 0)
        u_prev = jnp.where(r == 0, prev_edge, pltpu.roll(u, 1, 0))
        u_next = jnp.where(r == rows - 1, next_edge, pltpu.roll(u, rows - 1, 0))
        cw = cw_ref[...]
        conv = u_prev * cw[0:1] + u * cw[1:2] + u_next * cw[2:3]

        zc = _silu(proj(hx, O_GC, D_CONV)) * proj(hx, O_BC, D_CONV) * conv
        yc = _dot(zc, woc_ref[...])

        att = att_ref[0, :, 0].reshape(D_MLA, rows).T
        ym = _dot(gm * att, wom_ref[...])

        merged = s_conv * yc + s_mla * ym
        y = _dot(merged, wo_ref[...])
        res_ref[...] = DEEPNORM_ALPHA * x + gate * y


def _mixer(x, mods, att_t, wt, cw, woc, wom, wo, ln):
    bsz, n, d = x.shape
    assert MIX_TILE == Q_TILE and n % MIX_TILE == 0
    nt = n // MIX_TILE
    n_tiles = bsz * nt
    blocks8 = MIX_TILE // SUBLANES
    const = lambda t: (0, 0)
    one = pl.Buffered(1)

    def tile(t):
        t = jnp.minimum(t, n_tiles - 1)
        return t // nt, t % nt

    def x_map(t):
        b, j = tile(t)
        return b, j, 0

    def prev_map(t):
        b, j = tile(t)
        return b, jnp.maximum(j * blocks8 - 1, 0), 0

    def next_map(t):
        b, j = tile(t)
        return b, jnp.minimum((j + 1) * blocks8, n // SUBLANES - 1), 0

    def mod_map(t):
        return tile(t)[0], 0, 0

    def att_map(t):
        b, j = tile(t)
        return b, 0, j, 0, 0

    def out_map(t):
        t = jnp.maximum(t - 1, 0)
        return t // nt, t % nt, 0

    return pl.pallas_call(
        functools.partial(_mixer_kernel, tiles_per_seq=nt),
        grid=(n_tiles + 1,),
        in_specs=[
            pl.BlockSpec((1, MIX_TILE, d), x_map),
            pl.BlockSpec((1, SUBLANES, d), prev_map),
            pl.BlockSpec((1, SUBLANES, d), next_map),
            pl.BlockSpec((1, SUBLANES, d), mod_map),
            pl.BlockSpec((1, MLA_HEADS, 1, V_DIM, MIX_TILE), att_map),
            pl.BlockSpec(wt.shape, const, pipeline_mode=one),
            pl.BlockSpec(cw.shape, const, pipeline_mode=one),
            pl.BlockSpec(woc.shape, const, pipeline_mode=one),
            pl.BlockSpec(wom.shape, const, pipeline_mode=one),
            pl.BlockSpec(wo.shape, const, pipeline_mode=one),
            pl.BlockSpec(ln.shape, const, pipeline_mode=one),
        ],
        out_specs=pl.BlockSpec((1, MIX_TILE, d), out_map),
        out_shape=jax.ShapeDtypeStruct((bsz, n, d), F32),
        scratch_shapes=[pltpu.VMEM((MIX_TILE, d), F32)],
        compiler_params=pltpu.CompilerParams(dimension_semantics=("arbitrary",), vmem_limit_bytes=VMEM_LIMIT),
        name="mixer",
    )(x, x, x, mods, att_t, wt, cw, woc, wom, wo, ln)


def _rope_pair_swap():
    q = QK_ROPE // 4
    idx = np.concatenate([np.arange(q, 2 * q), np.arange(0, q), np.arange(3 * q, 4 * q), np.arange(2 * q, 3 * q)])
    sign = np.concatenate([-np.ones(q), np.ones(q), -np.ones(q), np.ones(q)]).astype(np.float32)
    return idx, sign


def _rope_tables(n, scale):
    rows = n // GRID_W
    row_pos = np.repeat(np.arange(rows), GRID_W).astype(np.float64)
    col_pos = np.tile(np.arange(GRID_W), rows).astype(np.float64)
    axis_dim = QK_ROPE // 2
    inv_freq = ROPE_THETA ** (-np.arange(0, axis_dim, 2, dtype=np.float64) / axis_dim)
    ang_r = row_pos[:, None] * inv_freq[None, :]
    ang_c = col_pos[:, None] * inv_freq[None, :]
    cos = np.concatenate([np.cos(ang_r), np.cos(ang_r), np.cos(ang_c), np.cos(ang_c)], axis=-1)
    sin = np.concatenate([np.sin(ang_r), np.sin(ang_r), np.sin(ang_c), np.sin(ang_c)], axis=-1)
    pad = np.zeros((n, HEAD_PAD - QK_NOPE - QK_ROPE))
    zero_n = np.zeros((n, QK_NOPE))
    cq = np.concatenate([np.full((n, QK_NOPE), scale), cos * scale, pad], axis=-1)
    sq = np.concatenate([zero_n, sin * scale, pad], axis=-1)
    ck = np.concatenate([zero_n, cos, pad], axis=-1)
    sk = np.concatenate([zero_n, sin, pad], axis=-1)
    return jnp.asarray(np.concatenate([cq, sq, ck, sk], axis=-1), dtype=F32)


def _place_rope_rows(w):
    d = w.shape[1]
    return jnp.concatenate([jnp.zeros((QK_NOPE, d), w.dtype), w,
                            jnp.zeros((HEAD_PAD - QK_NOPE - QK_ROPE, d), w.dtype)], axis=0)


def kernel(x, c, ctx, c_ctx, w_ada, b_ada, w_in, conv_w, q_norm_g, w_uq, kv_norm_g, w_ukv, w_out_conv,
           w_out_mla, w_o, ln_g, ln_b):
    bsz, n, d = x.shape
    assert w_ada.shape[0] == DEPTH == 1
    w_ada, b_ada, w_in, conv_w = w_ada[0], b_ada[0], w_in[0], conv_w[0]
    q_norm_g, w_uq, kv_norm_g, w_ukv = q_norm_g[0], w_uq[0], kv_norm_g[0], w_ukv[0]
    w_out_conv, w_out_mla, w_o, ln_g, ln_b = w_out_conv[0], w_out_mla[0], w_o[0], ln_g[0], ln_b[0]

    pad_rows = (-(bsz + 1)) % SUBLANES
    c_rows = jnp.concatenate([c, c_ctx[None, :], jnp.zeros((pad_rows, d), F32)], axis=0)
    mod = _adaln(c_rows, w_ada, b_ada[None, :])
    mods = mod[:bsz + 1].reshape(bsz + 1, 3, d)
    mods = jnp.concatenate([mods, jnp.zeros((bsz + 1, SUBLANES - 3, d), F32)], axis=1)

    wt = jnp.swapaxes(w_in, 0, 1)

    idx, sign = _rope_pair_swap()
    w_kr = wt[O_KR:O_KR + QK_ROPE]
    w_kr_rot = w_kr[idx] * sign[:, None]
    watt = jnp.concatenate([wt[O_CQ:O_KR], _place_rope_rows(w_kr), _place_rope_rows(w_kr_rot)], axis=0)

    hq = QK_NOPE + QK_ROPE
    w_uq_h = w_uq.reshape(Q_LORA, MLA_HEADS, hq)
    zq = jnp.zeros((Q_LORA, MLA_HEADS, HEAD_PAD - hq), F32)
    wq_plain = jnp.concatenate([w_uq_h, zq], axis=-1).reshape(Q_LORA, MLA_HEADS * HEAD_PAD)
    wq_rot = jnp.concatenate([jnp.zeros((Q_LORA, MLA_HEADS, QK_NOPE), F32),
                              w_uq_h[..., QK_NOPE + idx] * sign, zq], axis=-1).reshape(Q_LORA, MLA_HEADS * HEAD_PAD)
    wq = jnp.concatenate([wq_plain, wq_rot], axis=-1)

    w_ukv_h = w_ukv.reshape(KV_LORA, MLA_HEADS, QK_NOPE + V_DIM)
    wk = jnp.concatenate([w_ukv_h[..., :QK_NOPE], jnp.zeros((KV_LORA, MLA_HEADS, HEAD_PAD - QK_NOPE), F32)],
                         axis=-1).reshape(KV_LORA, MLA_HEADS * HEAD_PAD)
    wv = jnp.concatenate([w_ukv_h[..., QK_NOPE:], jnp.zeros((KV_LORA, MLA_HEADS, V_ROWS - V_DIM), F32)], axis=-1)
    wvt = wv.reshape(KV_LORA, MLA_HEADS * V_ROWS).T

    tab = _rope_tables(n, float((QK_NOPE + QK_ROPE) ** -0.5 * np.log2(np.e)))
    gq, gkv = q_norm_g[None, :], kv_norm_g[None, :]
    q, kx, vtx = _proj_x(x, mods, tab, watt, gq, gkv, wq, wk, wvt)
    kc, vtc = _proj_ctx(ctx, mods, watt, gkv, wk, wvt)
    att_t = _attn(q, kx, kc, vtx, vtc)

    cw = jnp.concatenate([conv_w, jnp.zeros((SUBLANES - conv_w.shape[0], D_CONV), F32)], axis=0)
    ln = jnp.concatenate([ln_g[None, :], ln_b[None, :], jnp.zeros((SUBLANES - 2, d), F32)], axis=0)
    return _mixer(x, mods, att_t, wt, cw, w_out_conv, w_out_mla, w_o, ln)
```

```python
import functools

import jax
import jax.numpy as jnp
import numpy as np
from jax import lax
from jax.experimental import pallas as pl
from jax.experimental.pallas import tpu as pltpu

GRID_W = 64
D_CONV = 512
MLA_HEADS = 8
QK_NOPE = 64
QK_ROPE = 32
V_DIM = 64
D_MLA = MLA_HEADS * V_DIM
Q_LORA = 256
KV_LORA = 128
ROPE_THETA = 10000.0
LN_EPS = 1e-5
RMS_EPS = 1e-6
DEPTH = 1
DEEPNORM_ALPHA = (2.0 * DEPTH) ** 0.25

O_XC, O_BC, O_CC, O_GC = 0, D_CONV, 2 * D_CONV, 3 * D_CONV
O_CQ = 4 * D_CONV
O_CKV = O_CQ + Q_LORA
O_KR = O_CKV + KV_LORA
O_GM = O_KR + QK_ROPE
O_GCONV = O_GM + D_MLA

LANES = 128
SUBLANES = 8
HEAD_PAD = LANES
V_ROWS = V_DIM + 16
PROJ_TILE = 512
MIX_TILE = 512
Q_TILE = 512
VMEM_LIMIT = 56 * 1024 * 1024

F32 = jnp.float32
BF16 = jnp.bfloat16


def _dot(a, b):
    return jnp.dot(a, b, preferred_element_type=F32)


def _dot_nt(a, b):
    return lax.dot_general(a, b, (((1,), (1,)), ((), ())), preferred_element_type=F32)


def _silu(t):
    return t * jax.nn.sigmoid(t)


def _rms_norm(t, g):
    return t * lax.rsqrt(jnp.mean(t * t, axis=-1, keepdims=True) + RMS_EPS) * g


def _adaln_kernel(c_ref, w_ref, b_ref, o_ref):
    cs = _silu(c_ref[...])
    o_ref[...] = _dot(cs, w_ref[...]) + b_ref[...]


def _adaln(c_rows, w_ada, b_ada):
    rows, d = c_rows.shape
    n_out = w_ada.shape[1]
    col_tile = 1024
    return pl.pallas_call(
        _adaln_kernel,
        grid=(n_out // col_tile,),
        in_specs=[
            pl.BlockSpec((rows, d), lambda i: (0, 0)),
            pl.BlockSpec((d, col_tile), lambda i: (0, i)),
            pl.BlockSpec((1, col_tile), lambda i: (0, i)),
        ],
        out_specs=pl.BlockSpec((rows, col_tile), lambda i: (0, i)),
        out_shape=jax.ShapeDtypeStruct((rows, n_out), F32),
        name="adaln",
    )(c_rows, w_ada, b_ada)


def _keys_values(ckv, kr_placed, gkv_ref, wk_ref, wvt_ref, k_ref, vt_ref):
    kvn = _rms_norm(ckv, gkv_ref[...])
    kn = _dot(kvn, wk_ref[...])
    for h in range(MLA_HEADS):
        k_ref[0, h] = (kn[:, h * HEAD_PAD:(h + 1) * HEAD_PAD] + kr_placed).astype(BF16)
    vt = _dot_nt(wvt_ref[...], kvn)
    r = lax.broadcasted_iota(jnp.int32, vt.shape, 0)
    ones = r == V_DIM
    for h in range(1, MLA_HEADS):
        ones = ones | (r == h * V_ROWS + V_DIM)
    vt = jnp.where(ones, 1.0, vt)
    vt_ref[0] = vt.astype(BF16).reshape(MLA_HEADS, V_ROWS, vt.shape[1])


def _proj_x_kernel(x_ref, mod_ref, tab_ref, watt_ref, gq_ref, gkv_ref, wq_ref, wk_ref, wvt_ref,
                   q_ref, k_ref, vt_ref):
    mod = mod_ref[0]
    h = x_ref[0] * (1.0 + mod[1:2]) + mod[0:1]
    p = _dot_nt(h, watt_ref[...])
    tab = tab_ref[...]
    cq_t, sq_t = tab[:, 0:LANES], tab[:, LANES:2 * LANES]
    ck_t, sk_t = tab[:, 2 * LANES:3 * LANES], tab[:, 3 * LANES:4 * LANES]
    kr0 = Q_LORA + KV_LORA
    kr_placed = p[:, kr0:kr0 + LANES] * ck_t + p[:, kr0 + LANES:kr0 + 2 * LANES] * sk_t
    _keys_values(p[:, Q_LORA:kr0], kr_placed, gkv_ref, wk_ref, wvt_ref, k_ref, vt_ref)
    qn = _rms_norm(p[:, 0:Q_LORA], gq_ref[...])
    qq = _dot(qn, wq_ref[...])
    width = MLA_HEADS * HEAD_PAD
    for hd in range(MLA_HEADS):
        a = qq[:, hd * HEAD_PAD:(hd + 1) * HEAD_PAD]
        b = qq[:, width + hd * HEAD_PAD:width + (hd + 1) * HEAD_PAD]
        q_ref[0, hd] = (a * cq_t + b * sq_t).astype(BF16)


def _proj_ctx_kernel(ctx_ref, mod_ref, watt_ref, gkv_ref, wk_ref, wvt_ref, k_ref, vt_ref):
    mod = mod_ref[0]
    h = ctx_ref[0] * (1.0 + mod[1:2]) + mod[0:1]
    p = _dot_nt(h, watt_ref[Q_LORA:Q_LORA + KV_LORA + LANES, :])
    _keys_values(p[:, 0:KV_LORA], p[:, KV_LORA:], gkv_ref, wk_ref, wvt_ref, k_ref, vt_ref)


def _proj_x(x, mods, tab, watt, gq, gkv, wq, wk, wvt):
    bsz, n, d = x.shape
    assert n % PROJ_TILE == 0
    const = lambda j, b: (0, 0)
    one = pl.Buffered(1)
    head_block = pl.BlockSpec((1, MLA_HEADS, PROJ_TILE, HEAD_PAD), lambda j, b: (b, 0, j, 0))
    return pl.pallas_call(
        _proj_x_kernel,
        grid=(n // PROJ_TILE, bsz),
        in_specs=[
            pl.BlockSpec((1, PROJ_TILE, d), lambda j, b: (b, j, 0)),
            pl.BlockSpec((1, SUBLANES, d), lambda j, b: (b, 0, 0)),
            pl.BlockSpec((PROJ_TILE, 4 * LANES), lambda j, b: (j, 0)),
            pl.BlockSpec(watt.shape, const, pipeline_mode=one),
            pl.BlockSpec(gq.shape, const, pipeline_mode=one),
            pl.BlockSpec(gkv.shape, const, pipeline_mode=one),
            pl.BlockSpec(wq.shape, const, pipeline_mode=one),
            pl.BlockSpec(wk.shape, const, pipeline_mode=one),
            pl.BlockSpec(wvt.shape, const, pipeline_mode=one),
        ],
        out_specs=[head_block, head_block,
                   pl.BlockSpec((1, MLA_HEADS, V_ROWS, PROJ_TILE), lambda j, b: (b, 0, 0, j))],
        out_shape=[
            jax.ShapeDtypeStruct((bsz, MLA_HEADS, n, HEAD_PAD), BF16),
            jax.ShapeDtypeStruct((bsz, MLA_HEADS, n, HEAD_PAD), BF16),
            jax.ShapeDtypeStruct((bsz, MLA_HEADS, V_ROWS, n), BF16),
        ],
        compiler_params=pltpu.CompilerParams(
            dimension_semantics=("arbitrary", "arbitrary"), vmem_limit_bytes=VMEM_LIMIT),
        name="proj_x",
    )(x, mods, tab, watt, gq, gkv, wq, wk, wvt)


def _proj_ctx(ctx, mods, watt, gkv, wk, wvt):
    bsz, n_ctx, d = ctx.shape
    const = lambda b: (0, 0)
    one = pl.Buffered(1)
    return pl.pallas_call(
        _proj_ctx_kernel,
        grid=(bsz,),
        in_specs=[
            pl.BlockSpec((1, n_ctx, d), lambda b: (b, 0, 0)),
            pl.BlockSpec((1, SUBLANES, d), lambda b: (bsz, 0, 0)),
            pl.BlockSpec(watt.shape, const, pipeline_mode=one),
            pl.BlockSpec(gkv.shape, const, pipeline_mode=one),
            pl.BlockSpec(wk.shape, const, pipeline_mode=one),
            pl.BlockSpec(wvt.shape, const, pipeline_mode=one),
        ],
        out_specs=[pl.BlockSpec((1, MLA_HEADS, n_ctx, HEAD_PAD), lambda b: (b, 0, 0, 0)),
                   pl.BlockSpec((1, MLA_HEADS, V_ROWS, n_ctx), lambda b: (b, 0, 0, 0))],
        out_shape=[
            jax.ShapeDtypeStruct((bsz, MLA_HEADS, n_ctx, HEAD_PAD), BF16),
            jax.ShapeDtypeStruct((bsz, MLA_HEADS, V_ROWS, n_ctx), BF16),
        ],
        compiler_params=pltpu.CompilerParams(dimension_semantics=("arbitrary",), vmem_limit_bytes=VMEM_LIMIT),
        name="proj_ctx",
    )(ctx, mods, watt, gkv, wk, wvt)


def _attn_kernel(q_ref, kx_ref, kc_ref, vtx_ref, vtc_ref, o_ref, s_ref):
    heads = q_ref.shape[1]
    nq = q_ref.shape[2] // Q_TILE
    nx = kx_ref.shape[2]

    def scores(h, i, slot):
        q = q_ref[0, h, i * Q_TILE:(i + 1) * Q_TILE, :]
        s_ref[slot, 0:nx] = _dot_nt(kx_ref[0, h], q)
        s_ref[slot, nx:] = _dot_nt(kc_ref[0, h], q)

    def finish(h, i, slot):
        s_t = s_ref[slot]
        m = jnp.max(s_t, axis=0, keepdims=True)
        p_t = jnp.exp2(s_t - m).astype(BF16)
        o_t = _dot(vtx_ref[0, h], p_t[0:nx]) + _dot(vtc_ref[0, h], p_t[nx:])
        o_ref[0, h, i] = o_t[0:V_DIM] * (1.0 / o_t[V_DIM:V_DIM + 1])

    def head(h, next_h):
        for i in range(nq):
            slot = i % 2
            if i + 1 < nq:
                scores(h, i + 1, 1 - slot)
            elif next_h is not None:
                scores(next_h, 0, 1 - slot)
            finish(h, i, slot)

    assert nq % 2 == 0
    scores(0, 0, 0)

    def body(h, carry):
        head(h, h + 1)
        return carry

    lax.fori_loop(0, heads - 1, body, 0)
    head(heads - 1, None)


def _attn(q, kx, kc, vtx, vtc):
    bsz, heads, n, _ = q.shape
    n_ctx = kc.shape[2]
    n_keys = n + n_ctx
    nq = n // Q_TILE
    whole = lambda b: (b, 0, 0, 0)
    return pl.pallas_call(
        _attn_kernel,
        grid=(bsz,),
        in_specs=[
            pl.BlockSpec((1, heads, n, HEAD_PAD), whole),
            pl.BlockSpec((1, heads, n, HEAD_PAD), whole),
            pl.BlockSpec((1, heads, n_ctx, HEAD_PAD), whole),
            pl.BlockSpec((1, heads, V_ROWS, n), whole),
            pl.BlockSpec((1, heads, V_ROWS, n_ctx), whole),
        ],
        out_specs=pl.BlockSpec((1, heads, nq, V_DIM, Q_TILE), lambda b: (b, 0, 0, 0, 0)),
        out_shape=jax.ShapeDtypeStruct((bsz, heads, nq, V_DIM, Q_TILE), F32),
        scratch_shapes=[pltpu.VMEM((2, n_keys, Q_TILE), F32)],
        compiler_params=pltpu.CompilerParams(
            dimension_semantics=("arbitrary",), vmem_limit_bytes=VMEM_LIMIT),
        name="attn",
    )(q, kx, kc, vtx, vtc)


def _mixer_kernel(x_ref, xp_ref, xn_ref, mod_ref, att_ref, wt_ref, cw_ref, woc_ref, wom_ref, wo_ref, ln_ref,
                  o_ref, res_ref, *, tiles_per_seq):
    t = pl.program_id(0)
    n_tiles = pl.num_programs(0) - 1

    def layer_norm():
        res = res_ref[...]
        mu = jnp.mean(res, axis=-1, keepdims=True)
        dev = res - mu
        var = jnp.mean(dev * dev, axis=-1, keepdims=True)
        ln = ln_ref[...]
        o_ref[0] = dev * lax.rsqrt(var + LN_EPS) * ln[0:1] + ln[1:2]

    @pl.when(t == 0)
    def _():
        res_ref[...] = jnp.zeros_like(res_ref)

    @pl.when(t == n_tiles)
    def _():
        layer_norm()

    @pl.when(t < n_tiles)
    def _():
        layer_norm()
        j = t % tiles_per_seq
        d = x_ref.shape[2]
        x = x_ref[0]
        rows = x.shape[0]
        mod = mod_ref[0]
        shift = mod[0:1]
        scale1 = 1.0 + mod[1:2]
        gate = mod[2:3]
        hx = x * scale1 + shift

        def proj(h, off, width):
            return _dot_nt(h, wt_ref[off:off + width, :])

        xh = jnp.concatenate([xp_ref[0], xn_ref[0]], axis=0)
        hh = xh * scale1 + shift
        hx_ext = jnp.concatenate([hx, hh], axis=0)
        u_ext = proj(hx_ext, O_XC, D_CONV) * proj(hx_ext, O_CC, D_CONV)
        u, uh = u_ext[:rows], u_ext[rows:]
        prev_edge = uh[SUBLANES - 1:SUBLANES] * (j > 0).astype(F32)
        next_edge = uh[SUBLANES:SUBLANES + 1] * (j < tiles_per_seq - 1).astype(F32)

        s_conv = jax.nn.sigmoid(proj(hx, O_GCONV, d))
        s_mla = jax.nn.sigmoid(proj(hx, O_GCONV + d, d))
        gm = _silu(proj(hx, O_GM, D_MLA))

        r = lax.broadcasted_iota(jnp.int32, u.shape, 0)
        u_prev = jnp.where(r == 0, prev_edge, pltpu.roll(u, 1, 0))
        u_next = jnp.where(r == rows - 1, next_edge, pltpu.roll(u, rows - 1, 0))
        cw = cw_ref[...]
        conv = u_prev * cw[0:1] + u * cw[1:2] + u_next * cw[2:3]

        zc = _silu(proj(hx, O_GC, D_CONV)) * proj(hx, O_BC, D_CONV) * conv
        yc = _dot(zc, woc_ref[...])

        att = att_ref[0, :, 0].reshape(D_MLA, rows).T
        ym = _dot(gm * att, wom_ref[...])

        merged = s_conv * yc + s_mla * ym
        y = _dot(merged, wo_ref[...])
        res_ref[...] = DEEPNORM_ALPHA * x + gate * y


def _mixer(x, mods, att_t, wt, cw, woc, wom, wo, ln):
    bsz, n, d = x.shape
    assert MIX_TILE == Q_TILE and n % MIX_TILE == 0
    nt = n // MIX_TILE
    n_tiles = bsz * nt
    blocks8 = MIX_TILE // SUBLANES
    const = lambda t: (0, 0)
    one = pl.Buffered(1)

    def tile(t):
        t = jnp.minimum(t, n_tiles - 1)
        return t // nt, t % nt

    def x_map(t):
        b, j = tile(t)
        return b, j, 0

    def prev_map(t):
        b, j = tile(t)
        return b, jnp.maximum(j * blocks8 - 1, 0), 0

    def next_map(t):
        b, j = tile(t)
        return b, jnp.minimum((j + 1) * blocks8, n // SUBLANES - 1), 0

    def mod_map(t):
        return tile(t)[0], 0, 0

    def att_map(t):
        b, j = tile(t)
        return b, 0, j, 0, 0

    def out_map(t):
        t = jnp.maximum(t - 1, 0)
        return t // nt, t % nt, 0

    return pl.pallas_call(
        functools.partial(_mixer_kernel, tiles_per_seq=nt),
        grid=(n_tiles + 1,),
        in_specs=[
            pl.BlockSpec((1, MIX_TILE, d), x_map),
            pl.BlockSpec((1, SUBLANES, d), prev_map),
            pl.BlockSpec((1, SUBLANES, d), next_map),
            pl.BlockSpec((1, SUBLANES, d), mod_map),
            pl.BlockSpec((1, MLA_HEADS, 1, V_DIM, MIX_TILE), att_map),
            pl.BlockSpec(wt.shape, const, pipeline_mode=one),
            pl.BlockSpec(cw.shape, const, pipeline_mode=one),
            pl.BlockSpec(woc.shape, const, pipeline_mode=one),
            pl.BlockSpec(wom.shape, const, pipeline_mode=one),
            pl.BlockSpec(wo.shape, const, pipeline_mode=one),
            pl.BlockSpec(ln.shape, const, pipeline_mode=one),
        ],
        out_specs=pl.BlockSpec((1, MIX_TILE, d), out_map),
        out_shape=jax.ShapeDtypeStruct((bsz, n, d), F32),
        scratch_shapes=[pltpu.VMEM((MIX_TILE, d), F32)],
        compiler_params=pltpu.CompilerParams(dimension_semantics=("arbitrary",), vmem_limit_bytes=VMEM_LIMIT),
        name="mixer",
    )(x, x, x, mods, att_t, wt, cw, woc, wom, wo, ln)


def _rope_pair_swap():
    q = QK_ROPE // 4
    idx = np.concatenate([np.arange(q, 2 * q), np.arange(0, q), np.arange(3 * q, 4 * q), np.arange(2 * q, 3 * q)])
    sign = np.concatenate([-np.ones(q), np.ones(q), -np.ones(q), np.ones(q)]).astype(np.float32)
    return idx, sign


def _rope_tables(n, scale):
    rows = n // GRID_W
    row_pos = np.repeat(np.arange(rows), GRID_W).astype(np.float64)
    col_pos = np.tile(np.arange(GRID_W), rows).astype(np.float64)
    axis_dim = QK_ROPE // 2
    inv_freq = ROPE_THETA ** (-np.arange(0, axis_dim, 2, dtype=np.float64) / axis_dim)
    ang_r = row_pos[:, None] * inv_freq[None, :]
    ang_c = col_pos[:, None] * inv_freq[None, :]
    cos = np.concatenate([np.cos(ang_r), np.cos(ang_r), np.cos(ang_c), np.cos(ang_c)], axis=-1)
    sin = np.concatenate([np.sin(ang_r), np.sin(ang_r), np.sin(ang_c), np.sin(ang_c)], axis=-1)
    pad = np.zeros((n, HEAD_PAD - QK_NOPE - QK_ROPE))
    zero_n = np.zeros((n, QK_NOPE))
    cq = np.concatenate([np.full((n, QK_NOPE), scale), cos * scale, pad], axis=-1)
    sq = np.concatenate([zero_n, sin * scale, pad], axis=-1)
    ck = np.concatenate([zero_n, cos, pad], axis=-1)
    sk = np.concatenate([zero_n, sin, pad], axis=-1)
    return jnp.asarray(np.concatenate([cq, sq, ck, sk], axis=-1), dtype=F32)


def _place_rope_rows(w):
    d = w.shape[1]
    return jnp.concatenate([jnp.zeros((QK_NOPE, d), w.dtype), w,
                            jnp.zeros((HEAD_PAD - QK_NOPE - QK_ROPE, d), w.dtype)], axis=0)


def kernel(x, c, ctx, c_ctx, w_ada, b_ada, w_in, conv_w, q_norm_g, w_uq, kv_norm_g, w_ukv, w_out_conv,
           w_out_mla, w_o, ln_g, ln_b):
    bsz, n, d = x.shape
    assert w_ada.shape[0] == DEPTH == 1
    w_ada, b_ada, w_in, conv_w = w_ada[0], b_ada[0], w_in[0], conv_w[0]
    q_norm_g, w_uq, kv_norm_g, w_ukv = q_norm_g[0], w_uq[0], kv_norm_g[0], w_ukv[0]
    w_out_conv, w_out_mla, w_o, ln_g, ln_b = w_out_conv[0], w_out_mla[0], w_o[0], ln_g[0], ln_b[0]

    pad_rows = (-(bsz + 1)) % SUBLANES
    c_rows = jnp.concatenate([c, c_ctx[None, :], jnp.zeros((pad_rows, d), F32)], axis=0)
    mod = _adaln(c_rows, w_ada, b_ada[None, :])
    mods = mod[:bsz + 1].reshape(bsz + 1, 3, d)
    mods = jnp.concatenate([mods, jnp.zeros((bsz + 1, SUBLANES - 3, d), F32)], axis=1)

    wt = jnp.swapaxes(w_in, 0, 1)

    idx, sign = _rope_pair_swap()
    w_kr = wt[O_KR:O_KR + QK_ROPE]
    w_kr_rot = w_kr[idx] * sign[:, None]
    watt = jnp.concatenate([wt[O_CQ:O_KR], _place_rope_rows(w_kr), _place_rope_rows(w_kr_rot)], axis=0)

    hq = QK_NOPE + QK_ROPE
    w_uq_h = w_uq.reshape(Q_LORA, MLA_HEADS, hq)
    zq = jnp.zeros((Q_LORA, MLA_HEADS, HEAD_PAD - hq), F32)
    wq_plain = jnp.concatenate([w_uq_h, zq], axis=-1).reshape(Q_LORA, MLA_HEADS * HEAD_PAD)
    wq_rot = jnp.concatenate([jnp.zeros((Q_LORA, MLA_HEADS, QK_NOPE), F32),
                              w_uq_h[..., QK_NOPE + idx] * sign, zq], axis=-1).reshape(Q_LORA, MLA_HEADS * HEAD_PAD)
    wq = jnp.concatenate([wq_plain, wq_rot], axis=-1)

    w_ukv_h = w_ukv.reshape(KV_LORA, MLA_HEADS, QK_NOPE + V_DIM)
    wk = jnp.concatenate([w_ukv_h[..., :QK_NOPE], jnp.zeros((KV_LORA, MLA_HEADS, HEAD_PAD - QK_NOPE), F32)],
                         axis=-1).reshape(KV_LORA, MLA_HEADS * HEAD_PAD)
    wv = jnp.concatenate([w_ukv_h[..., QK_NOPE:], jnp.zeros((KV_LORA, MLA_HEADS, V_ROWS - V_DIM), F32)], axis=-1)
    wvt = wv.reshape(KV_LORA, MLA_HEADS * V_ROWS).T

    tab = _rope_tables(n, float((QK_NOPE + QK_ROPE) ** -0.5 * np.log2(np.e)))
    gq, gkv = q_norm_g[None, :], kv_norm_g[None, :]
    q, kx, vtx = _proj_x(x, mods, tab, watt, gq, gkv, wq, wk, wvt)
    kc, vtc = _proj_ctx(ctx, mods, watt, gkv, wk, wvt)
    att_t = _attn(q, kx, kc, vtx, vtc)

    cw = jnp.concatenate([conv_w, jnp.zeros((SUBLANES - conv_w.shape[0], D_CONV), F32)], axis=0)
    ln = jnp.concatenate([ln_g[None, :], ln_b[None, :], jnp.zeros((SUBLANES - 2, d), F32)], axis=0)
    return _mixer(x, mods, att_t, wt, cw, w_out_conv, w_out_mla, w_o, ln)
```

```python
import functools

import jax
import jax.numpy as jnp
import numpy as np
from jax import lax
from jax.experimental import pallas as pl
from jax.experimental.pallas import tpu as pltpu

GRID_W = 64
D_CONV = 512
MLA_HEADS = 8
QK_NOPE = 64
QK_ROPE = 32
V_DIM = 64
D_MLA = MLA_HEADS * V_DIM
Q_LORA = 256
KV_LORA = 128
ROPE_THETA = 10000.0
LN_EPS = 1e-5
RMS_EPS = 1e-6
DEPTH = 1
DEEPNORM_ALPHA = (2.0 * DEPTH) ** 0.25

O_XC, O_BC, O_CC, O_GC = 0, D_CONV, 2 * D_CONV, 3 * D_CONV
O_CQ = 4 * D_CONV
O_CKV = O_CQ + Q_LORA
O_KR = O_CKV + KV_LORA
O_GM = O_KR + QK_ROPE
O_GCONV = O_GM + D_MLA

LANES = 128
SUBLANES = 8
HEAD_PAD = LANES
V_ROWS = V_DIM + 16
PROJ_TILE = 1024
MIX_TILE = 512
Q_TILE = 512
VMEM_LIMIT = 56 * 1024 * 1024

F32 = jnp.float32
BF16 = jnp.bfloat16


def _dot(a, b):
    return jnp.dot(a, b, preferred_element_type=F32)


def _dot_nt(a, b):
    return lax.dot_general(a, b, (((1,), (1,)), ((), ())), preferred_element_type=F32)


def _silu(t):
    return t * jax.nn.sigmoid(t)


def _rms_norm(t, g):
    return t * lax.rsqrt(jnp.mean(t * t, axis=-1, keepdims=True) + RMS_EPS) * g


def _adaln_kernel(c_ref, w_ref, b_ref, o_ref):
    cs = _silu(c_ref[...])
    o_ref[...] = _dot(cs, w_ref[...]) + b_ref[...]


def _adaln(c_rows, w_ada, b_ada):
    rows, d = c_rows.shape
    n_out = w_ada.shape[1]
    col_tile = 1024
    return pl.pallas_call(
        _adaln_kernel,
        grid=(n_out // col_tile,),
        in_specs=[
            pl.BlockSpec((rows, d), lambda i: (0, 0)),
            pl.BlockSpec((d, col_tile), lambda i: (0, i)),
            pl.BlockSpec((1, col_tile), lambda i: (0, i)),
        ],
        out_specs=pl.BlockSpec((rows, col_tile), lambda i: (0, i)),
        out_shape=jax.ShapeDtypeStruct((rows, n_out), F32),
        name="adaln",
    )(c_rows, w_ada, b_ada)


def _keys_values(ckv, kr_placed, gkv_ref, wk_ref, wvt_ref, k_ref, vt_ref):
    kvn = _rms_norm(ckv, gkv_ref[...])
    kn = _dot(kvn, wk_ref[...])
    for h in range(MLA_HEADS):
        k_ref[0, h] = (kn[:, h * HEAD_PAD:(h + 1) * HEAD_PAD] + kr_placed).astype(BF16)
    vt = _dot_nt(wvt_ref[...], kvn)
    r = lax.broadcasted_iota(jnp.int32, vt.shape, 0)
    ones = r == V_DIM
    for h in range(1, MLA_HEADS):
        ones = ones | (r == h * V_ROWS + V_DIM)
    vt = jnp.where(ones, 1.0, vt)
    vt_ref[0] = vt.astype(BF16).reshape(MLA_HEADS, V_ROWS, vt.shape[1])


def _proj_x_kernel(x_ref, mod_ref, tab_ref, watt_ref, gq_ref, gkv_ref, wq_ref, wk_ref, wvt_ref,
                   q_ref, k_ref, vt_ref):
    mod = mod_ref[0]
    h = x_ref[0] * (1.0 + mod[1:2]) + mod[0:1]
    p = _dot_nt(h, watt_ref[...])
    tab = tab_ref[...]
    cq_t, sq_t = tab[:, 0:LANES], tab[:, LANES:2 * LANES]
    ck_t, sk_t = tab[:, 2 * LANES:3 * LANES], tab[:, 3 * LANES:4 * LANES]
    kr0 = Q_LORA + KV_LORA
    kr_placed = p[:, kr0:kr0 + LANES] * ck_t + p[:, kr0 + LANES:kr0 + 2 * LANES] * sk_t
    _keys_values(p[:, Q_LORA:kr0], kr_placed, gkv_ref, wk_ref, wvt_ref, k_ref, vt_ref)
    qn = _rms_norm(p[:, 0:Q_LORA], gq_ref[...])
    qq = _dot(qn, wq_ref[...])
    width = MLA_HEADS * HEAD_PAD
    for hd in range(MLA_HEADS):
        a = qq[:, hd * HEAD_PAD:(hd + 1) * HEAD_PAD]
        b = qq[:, width + hd * HEAD_PAD:width + (hd + 1) * HEAD_PAD]
        q_ref[0, hd] = (a * cq_t + b * sq_t).astype(BF16)


def _proj_ctx_kernel(ctx_ref, mod_ref, watt_ref, gkv_ref, wk_ref, wvt_ref, k_ref, vt_ref):
    mod = mod_ref[0]
    h = ctx_ref[0] * (1.0 + mod[1:2]) + mod[0:1]
    p = _dot_nt(h, watt_ref[Q_LORA:Q_LORA + KV_LORA + LANES, :])
    _keys_values(p[:, 0:KV_LORA], p[:, KV_LORA:], gkv_ref, wk_ref, wvt_ref, k_ref, vt_ref)


def _proj_x(x, mods, tab, watt, gq, gkv, wq, wk, wvt):
    bsz, n, d = x.shape
    assert n % PROJ_TILE == 0
    const = lambda j, b: (0, 0)
    one = pl.Buffered(1)
    head_block = pl.BlockSpec((1, MLA_HEADS, PROJ_TILE, HEAD_PAD), lambda j, b: (b, 0, j, 0))
    return pl.pallas_call(
        _proj_x_kernel,
        grid=(n // PROJ_TILE, bsz),
        in_specs=[
            pl.BlockSpec((1, PROJ_TILE, d), lambda j, b: (b, j, 0)),
            pl.BlockSpec((1, SUBLANES, d), lambda j, b: (b, 0, 0)),
            pl.BlockSpec((PROJ_TILE, 4 * LANES), lambda j, b: (j, 0)),
            pl.BlockSpec(watt.shape, const, pipeline_mode=one),
            pl.BlockSpec(gq.shape, const, pipeline_mode=one),
            pl.BlockSpec(gkv.shape, const, pipeline_mode=one),
            pl.BlockSpec(wq.shape, const, pipeline_mode=one),
            pl.BlockSpec(wk.shape, const, pipeline_mode=one),
            pl.BlockSpec(wvt.shape, const, pipeline_mode=one),
        ],
        out_specs=[head_block, head_block,
                   pl.BlockSpec((1, MLA_HEADS, V_ROWS, PROJ_TILE), lambda j, b: (b, 0, 0, j))],
        out_shape=[
            jax.ShapeDtypeStruct((bsz, MLA_HEADS, n, HEAD_PAD), BF16),
            jax.ShapeDtypeStruct((bsz, MLA_HEADS, n, HEAD_PAD), BF16),
            jax.ShapeDtypeStruct((bsz, MLA_HEADS, V_ROWS, n), BF16),
        ],
        compiler_params=pltpu.CompilerParams(
            dimension_semantics=("arbitrary", "arbitrary"), vmem_limit_bytes=VMEM_LIMIT),
        name="proj_x",
    )(x, mods, tab, watt, gq, gkv, wq, wk, wvt)


def _proj_ctx(ctx, mods, watt, gkv, wk, wvt):
    bsz, n_ctx, d = ctx.shape
    const = lambda b: (0, 0)
    one = pl.Buffered(1)
    return pl.pallas_call(
        _proj_ctx_kernel,
        grid=(bsz,),
        in_specs=[
            pl.BlockSpec((1, n_ctx, d), lambda b: (b, 0, 0)),
            pl.BlockSpec((1, SUBLANES, d), lambda b: (bsz, 0, 0)),
            pl.BlockSpec(watt.shape, const, pipeline_mode=one),
            pl.BlockSpec(gkv.shape, const, pipeline_mode=one),
            pl.BlockSpec(wk.shape, const, pipeline_mode=one),
            pl.BlockSpec(wvt.shape, const, pipeline_mode=one),
        ],
        out_specs=[pl.BlockSpec((1, MLA_HEADS, n_ctx, HEAD_PAD), lambda b: (b, 0, 0, 0)),
                   pl.BlockSpec((1, MLA_HEADS, V_ROWS, n_ctx), lambda b: (b, 0, 0, 0))],
        out_shape=[
            jax.ShapeDtypeStruct((bsz, MLA_HEADS, n_ctx, HEAD_PAD), BF16),
            jax.ShapeDtypeStruct((bsz, MLA_HEADS, V_ROWS, n_ctx), BF16),
        ],
        compiler_params=pltpu.CompilerParams(dimension_semantics=("arbitrary",), vmem_limit_bytes=VMEM_LIMIT),
        name="proj_ctx",
    )(ctx, mods, watt, gkv, wk, wvt)


def _attn_kernel(q_ref, kx_ref, kc_ref, vtx_ref, vtc_ref, o_ref, s_ref):
    heads = q_ref.shape[1]
    nq = q_ref.shape[2] // Q_TILE
    nx = kx_ref.shape[2]

    def scores(h, i, slot):
        q = q_ref[0, h, i * Q_TILE:(i + 1) * Q_TILE, :]
        s_ref[slot, 0:nx] = _dot_nt(kx_ref[0, h], q)
        s_ref[slot, nx:] = _dot_nt(kc_ref[0, h], q)

    def finish(h, i, slot):
        s_t = s_ref[slot]
        m = jnp.max(s_t, axis=0, keepdims=True)
        p_t = jnp.exp2(s_t - m).astype(BF16)
        o_t = _dot(vtx_ref[0, h], p_t[0:nx]) + _dot(vtc_ref[0, h], p_t[nx:])
        o_ref[0, h, i] = o_t[0:V_DIM] * (1.0 / o_t[V_DIM:V_DIM + 1])

    def head(h, next_h):
        for i in range(nq):
            slot = i % 2
            if i + 1 < nq:
                scores(h, i + 1, 1 - slot)
            elif next_h is not None:
                scores(next_h, 0, 1 - slot)
            finish(h, i, slot)

    assert nq % 2 == 0
    scores(0, 0, 0)

    def body(h, carry):
        head(h, h + 1)
        return carry

    lax.fori_loop(0, heads - 1, body, 0)
    head(heads - 1, None)


def _attn(q, kx, kc, vtx, vtc):
    bsz, heads, n, _ = q.shape
    n_ctx = kc.shape[2]
    n_keys = n + n_ctx
    nq = n // Q_TILE
    whole = lambda b: (b, 0, 0, 0)
    return pl.pallas_call(
        _attn_kernel,
        grid=(bsz,),
        in_specs=[
            pl.BlockSpec((1, heads, n, HEAD_PAD), whole),
            pl.BlockSpec((1, heads, n, HEAD_PAD), whole),
            pl.BlockSpec((1, heads, n_ctx, HEAD_PAD), whole),
            pl.BlockSpec((1, heads, V_ROWS, n), whole),
            pl.BlockSpec((1, heads, V_ROWS, n_ctx), whole),
        ],
        out_specs=pl.BlockSpec((1, heads, nq, V_DIM, Q_TILE), lambda b: (b, 0, 0, 0, 0)),
        out_shape=jax.ShapeDtypeStruct((bsz, heads, nq, V_DIM, Q_TILE), F32),
        scratch_shapes=[pltpu.VMEM((2, n_keys, Q_TILE), F32)],
        compiler_params=pltpu.CompilerParams(
            dimension_semantics=("arbitrary",), vmem_limit_bytes=VMEM_LIMIT),
        name="attn",
    )(q, kx, kc, vtx, vtc)


def _mixer_kernel(x_ref, xp_ref, xn_ref, mod_ref, att_ref, wt_ref, cw_ref, woc_ref, wom_ref, wo_ref, ln_ref,
                  o_ref, res_ref, *, tiles_per_seq):
    t = pl.program_id(0)
    n_tiles = pl.num_programs(0) - 1

    def layer_norm():
        res = res_ref[...]
        mu = jnp.mean(res, axis=-1, keepdims=True)
        dev = res - mu
        var = jnp.mean(dev * dev, axis=-1, keepdims=True)
        ln = ln_ref[...]
        o_ref[0] = dev * lax.rsqrt(var + LN_EPS) * ln[0:1] + ln[1:2]

    @pl.when(t == 0)
    def _():
        res_ref[...] = jnp.zeros_like(res_ref)

    @pl.when(t == n_tiles)
    def _():
        layer_norm()

    @pl.when(t < n_tiles)
    def _():
        layer_norm()
        j = t % tiles_per_seq
        d = x_ref.shape[2]
        x = x_ref[0]
        rows = x.shape[0]
        mod = mod_ref[0]
        shift = mod[0:1]
        scale1 = 1.0 + mod[1:2]
        gate = mod[2:3]
        hx = x * scale1 + shift

        def proj(h, off, width):
            return _dot_nt(h, wt_ref[off:off + width, :])

        xh = jnp.concatenate([xp_ref[0], xn_ref[0]], axis=0)
        hh = xh * scale1 + shift
        hx_ext = jnp.concatenate([hx, hh], axis=0)
        u_ext = proj(hx_ext, O_XC, D_CONV) * proj(hx_ext, O_CC, D_CONV)
        u, uh = u_ext[:rows], u_ext[rows:]
        prev_edge = uh[SUBLANES - 1:SUBLANES] * (j > 0).astype(F32)
        next_edge = uh[SUBLANES:SUBLANES + 1] * (j < tiles_per_seq - 1).astype(F32)

        s_conv = jax.nn.sigmoid(proj(hx, O_GCONV, d))
        s_mla = jax.nn.sigmoid(proj(hx, O_GCONV + d, d))
        gm = _silu(proj(hx, O_GM, D_MLA))

        r = lax.broadcasted_iota(jnp.int32, u.shape, 0)
        u_prev = jnp.where(r == 0, prev_edge, pltpu.roll(u, 1, 0))
        u_next = jnp.where(r == rows - 1, next_edge, pltpu.roll(u, rows - 1, 0))
        cw = cw_ref[...]
        conv = u_prev * cw[0:1] + u * cw[1:2] + u_next * cw[2:3]

        zc = _silu(proj(hx, O_GC, D_CONV)) * proj(hx, O_BC, D_CONV) * conv
        yc = _dot(zc, woc_ref[...])

        att = att_ref[0, :, 0].reshape(D_MLA, rows).T
        ym = _dot(gm * att, wom_ref[...])

        merged = s_conv * yc + s_mla * ym
        y = _dot(merged, wo_ref[...])
        res_ref[...] = DEEPNORM_ALPHA * x + gate * y


def _mixer(x, mods, att_t, wt, cw, woc, wom, wo, ln):
    bsz, n, d = x.shape
    assert MIX_TILE == Q_TILE and n % MIX_TILE == 0
    nt = n // MIX_TILE
    n_tiles = bsz * nt
    blocks8 = MIX_TILE // SUBLANES
    const = lambda t: (0, 0)
    one = pl.Buffered(1)

    def tile(t):
        t = jnp.minimum(t, n_tiles - 1)
        return t // nt, t % nt

    def x_map(t):
        b, j = tile(t)
        return b, j, 0

    def prev_map(t):
        b, j = tile(t)
        return b, jnp.maximum(j * blocks8 - 1, 0), 0

    def next_map(t):
        b, j = tile(t)
        return b, jnp.minimum((j + 1) * blocks8, n // SUBLANES - 1), 0

    def mod_map(t):
        return tile(t)[0], 0, 0

    def att_map(t):
        b, j = tile(t)
        return b, 0, j, 0, 0

    def out_map(t):
        t = jnp.maximum(t - 1, 0)
        return t // nt, t % nt, 0

    return pl.pallas_call(
        functools.partial(_mixer_kernel, tiles_per_seq=nt),
        grid=(n_tiles + 1,),
        in_specs=[
            pl.BlockSpec((1, MIX_TILE, d), x_map),
            pl.BlockSpec((1, SUBLANES, d), prev_map),
            pl.BlockSpec((1, SUBLANES, d), next_map),
            pl.BlockSpec((1, SUBLANES, d), mod_map),
            pl.BlockSpec((1, MLA_HEADS, 1, V_DIM, MIX_TILE), att_map),
            pl.BlockSpec(wt.shape, const, pipeline_mode=one),
            pl.BlockSpec(cw.shape, const, pipeline_mode=one),
            pl.BlockSpec(woc.shape, const, pipeline_mode=one),
            pl.BlockSpec(wom.shape, const, pipeline_mode=one),
            pl.BlockSpec(wo.shape, const, pipeline_mode=one),
            pl.BlockSpec(ln.shape, const, pipeline_mode=one),
        ],
        out_specs=pl.BlockSpec((1, MIX_TILE, d), out_map),
        out_shape=jax.ShapeDtypeStruct((bsz, n, d), F32),
        scratch_shapes=[pltpu.VMEM((MIX_TILE, d), F32)],
        compiler_params=pltpu.CompilerParams(dimension_semantics=("arbitrary",), vmem_limit_bytes=VMEM_LIMIT),
        name="mixer",
    )(x, x, x, mods, att_t, wt, cw, woc, wom, wo, ln)


def _rope_pair_swap():
    q = QK_ROPE // 4
    idx = np.concatenate([np.arange(q, 2 * q), np.arange(0, q), np.arange(3 * q, 4 * q), np.arange(2 * q, 3 * q)])
    sign = np.concatenate([-np.ones(q), np.ones(q), -np.ones(q), np.ones(q)]).astype(np.float32)
    return idx, sign


def _rope_tables(n, scale):
    rows = n // GRID_W
    row_pos = np.repeat(np.arange(rows), GRID_W).astype(np.float64)
    col_pos = np.tile(np.arange(GRID_W), rows).astype(np.float64)
    axis_dim = QK_ROPE // 2
    inv_freq = ROPE_THETA ** (-np.arange(0, axis_dim, 2, dtype=np.float64) / axis_dim)
    ang_r = row_pos[:, None] * inv_freq[None, :]
    ang_c = col_pos[:, None] * inv_freq[None, :]
    cos = np.concatenate([np.cos(ang_r), np.cos(ang_r), np.cos(ang_c), np.cos(ang_c)], axis=-1)
    sin = np.concatenate([np.sin(ang_r), np.sin(ang_r), np.sin(ang_c), np.sin(ang_c)], axis=-1)
    pad = np.zeros((n, HEAD_PAD - QK_NOPE - QK_ROPE))
    zero_n = np.zeros((n, QK_NOPE))
    cq = np.concatenate([np.full((n, QK_NOPE), scale), cos * scale, pad], axis=-1)
    sq = np.concatenate([zero_n, sin * scale, pad], axis=-1)
    ck = np.concatenate([zero_n, cos, pad], axis=-1)
    sk = np.concatenate([zero_n, sin, pad], axis=-1)
    return jnp.asarray(np.concatenate([cq, sq, ck, sk], axis=-1), dtype=F32)


def _place_rope_rows(w):
    d = w.shape[1]
    return jnp.concatenate([jnp.zeros((QK_NOPE, d), w.dtype), w,
                            jnp.zeros((HEAD_PAD - QK_NOPE - QK_ROPE, d), w.dtype)], axis=0)


def kernel(x, c, ctx, c_ctx, w_ada, b_ada, w_in, conv_w, q_norm_g, w_uq, kv_norm_g, w_ukv, w_out_conv,
           w_out_mla, w_o, ln_g, ln_b):
    bsz, n, d = x.shape
    assert w_ada.shape[0] == DEPTH == 1
    w_ada, b_ada, w_in, conv_w = w_ada[0], b_ada[0], w_in[0], conv_w[0]
    q_norm_g, w_uq, kv_norm_g, w_ukv = q_norm_g[0], w_uq[0], kv_norm_g[0], w_ukv[0]
    w_out_conv, w_out_mla, w_o, ln_g, ln_b = w_out_conv[0], w_out_mla[0], w_o[0], ln_g[0], ln_b[0]

    pad_rows = (-(bsz + 1)) % SUBLANES
    c_rows = jnp.concatenate([c, c_ctx[None, :], jnp.zeros((pad_rows, d), F32)], axis=0)
    mod = _adaln(c_rows, w_ada, b_ada[None, :])
    mods = mod[:bsz + 1].reshape(bsz + 1, 3, d)
    mods = jnp.concatenate([mods, jnp.zeros((bsz + 1, SUBLANES - 3, d), F32)], axis=1)

    wt = jnp.swapaxes(w_in, 0, 1)

    idx, sign = _rope_pair_swap()
    w_kr = wt[O_KR:O_KR + QK_ROPE]
    w_kr_rot = w_kr[idx] * sign[:, None]
    watt = jnp.concatenate([wt[O_CQ:O_KR], _place_rope_rows(w_kr), _place_rope_rows(w_kr_rot)], axis=0)

    hq = QK_NOPE + QK_ROPE
    w_uq_h = w_uq.reshape(Q_LORA, MLA_HEADS, hq)
    zq = jnp.zeros((Q_LORA, MLA_HEADS, HEAD_PAD - hq), F32)
    wq_plain = jnp.concatenate([w_uq_h, zq], axis=-1).reshape(Q_LORA, MLA_HEADS * HEAD_PAD)
    wq_rot = jnp.concatenate([jnp.zeros((Q_LORA, MLA_HEADS, QK_NOPE), F32),
                              w_uq_h[..., QK_NOPE + idx] * sign, zq], axis=-1).reshape(Q_LORA, MLA_HEADS * HEAD_PAD)
    wq = jnp.concatenate([wq_plain, wq_rot], axis=-1)

    w_ukv_h = w_ukv.reshape(KV_LORA, MLA_HEADS, QK_NOPE + V_DIM)
    wk = jnp.concatenate([w_ukv_h[..., :QK_NOPE], jnp.zeros((KV_LORA, MLA_HEADS, HEAD_PAD - QK_NOPE), F32)],
                         axis=-1).reshape(KV_LORA, MLA_HEADS * HEAD_PAD)
    wv = jnp.concatenate([w_ukv_h[..., QK_NOPE:], jnp.zeros((KV_LORA, MLA_HEADS, V_ROWS - V_DIM), F32)], axis=-1)
    wvt = wv.reshape(KV_LORA, MLA_HEADS * V_ROWS).T

    tab = _rope_tables(n, float((QK_NOPE + QK_ROPE) ** -0.5 * np.log2(np.e)))
    gq, gkv = q_norm_g[None, :], kv_norm_g[None, :]
    q, kx, vtx = _proj_x(x, mods, tab, watt, gq, gkv, wq, wk, wvt)
    kc, vtc = _proj_ctx(ctx, mods, watt, gkv, wk, wvt)
    att_t = _attn(q, kx, kc, vtx, vtc)

    cw = jnp.concatenate([conv_w, jnp.zeros((SUBLANES - conv_w.shape[0], D_CONV), F32)], axis=0)
    ln = jnp.concatenate([ln_g[None, :], ln_b[None, :], jnp.zeros((SUBLANES - 2, d), F32)], axis=0)
    return _mixer(x, mods, att_t, wt, cw, w_out_conv, w_out_mla, w_o, ln)
```

```python
import functools

import jax
import jax.numpy as jnp
import numpy as np
from jax import lax
from jax.experimental import pallas as pl
from jax.experimental.pallas import tpu as pltpu

GRID_W = 64
D_CONV = 512
MLA_HEADS = 8
QK_NOPE = 64
QK_ROPE = 32
V_DIM = 64
D_MLA = MLA_HEADS * V_DIM
Q_LORA = 256
KV_LORA = 128
ROPE_THETA = 10000.0
LN_EPS = 1e-5
RMS_EPS = 1e-6
DEPTH = 1
DEEPNORM_ALPHA = (2.0 * DEPTH) ** 0.25

O_XC, O_BC, O_CC, O_GC = 0, D_CONV, 2 * D_CONV, 3 * D_CONV
O_CQ = 4 * D_CONV
O_CKV = O_CQ + Q_LORA
O_KR = O_CKV + KV_LORA
O_GM = O_KR + QK_ROPE
O_GCONV = O_GM + D_MLA

LANES = 128
SUBLANES = 8
HEAD_PAD = LANES
V_ROWS = V_DIM + 16
PROJ_TILE = 1024
CTX_BATCH = 4
MIX_TILE = 512
Q_TILE = 512
VMEM_LIMIT = 56 * 1024 * 1024

F32 = jnp.float32
BF16 = jnp.bfloat16


def _dot(a, b):
    return jnp.dot(a, b, preferred_element_type=F32)


def _dot_nt(a, b):
    return lax.dot_general(a, b, (((1,), (1,)), ((), ())), preferred_element_type=F32)


def _silu(t):
    return t * jax.nn.sigmoid(t)


def _rms_norm(t, g):
    return t * lax.rsqrt(jnp.mean(t * t, axis=-1, keepdims=True) + RMS_EPS) * g


def _adaln_kernel(c_ref, w_ref, b_ref, o_ref):
    cs = _silu(c_ref[...])
    o_ref[...] = _dot(cs, w_ref[...]) + b_ref[...]


def _adaln(c_rows, w_ada, b_ada):
    rows, d = c_rows.shape
    n_out = w_ada.shape[1]
    col_tile = 1024
    return pl.pallas_call(
        _adaln_kernel,
        grid=(n_out // col_tile,),
        in_specs=[
            pl.BlockSpec((rows, d), lambda i: (0, 0)),
            pl.BlockSpec((d, col_tile), lambda i: (0, i)),
            pl.BlockSpec((1, col_tile), lambda i: (0, i)),
        ],
        out_specs=pl.BlockSpec((rows, col_tile), lambda i: (0, i)),
        out_shape=jax.ShapeDtypeStruct((rows, n_out), F32),
        name="adaln",
    )(c_rows, w_ada, b_ada)


def _keys_values(ckv, kr_placed, gkv_ref, wk_ref, wvt_ref, k_ref, vt_ref):
    groups, per = k_ref.shape[0], k_ref.shape[2]
    kvn = _rms_norm(ckv, gkv_ref[...])
    kn = _dot(kvn, wk_ref[...])
    vt = _dot_nt(wvt_ref[...], kvn)
    r = lax.broadcasted_iota(jnp.int32, vt.shape, 0)
    ones = r == V_DIM
    for h in range(1, MLA_HEADS):
        ones = ones | (r == h * V_ROWS + V_DIM)
    vt = jnp.where(ones, 1.0, vt).astype(BF16)
    for g in range(groups):
        rows = slice(g * per, (g + 1) * per)
        for h in range(MLA_HEADS):
            k_ref[g, h] = (kn[rows, h * HEAD_PAD:(h + 1) * HEAD_PAD] + kr_placed[rows]).astype(BF16)
        vt_ref[g] = vt[:, rows].reshape(MLA_HEADS, V_ROWS, per)


def _proj_x_kernel(x_ref, mod_ref, tab_ref, watt_ref, gq_ref, gkv_ref, wq_ref, wk_ref, wvt_ref,
                   q_ref, k_ref, vt_ref):
    mod = mod_ref[0]
    h = x_ref[0] * (1.0 + mod[1:2]) + mod[0:1]
    p = _dot_nt(h, watt_ref[...])
    tab = tab_ref[...]
    cq_t, sq_t = tab[:, 0:LANES], tab[:, LANES:2 * LANES]
    ck_t, sk_t = tab[:, 2 * LANES:3 * LANES], tab[:, 3 * LANES:4 * LANES]
    kr0 = Q_LORA + KV_LORA
    kr_placed = p[:, kr0:kr0 + LANES] * ck_t + p[:, kr0 + LANES:kr0 + 2 * LANES] * sk_t
    _keys_values(p[:, Q_LORA:kr0], kr_placed, gkv_ref, wk_ref, wvt_ref, k_ref, vt_ref)
    qn = _rms_norm(p[:, 0:Q_LORA], gq_ref[...])
    qq = _dot(qn, wq_ref[...])
    width = MLA_HEADS * HEAD_PAD
    for hd in range(MLA_HEADS):
        a = qq[:, hd * HEAD_PAD:(hd + 1) * HEAD_PAD]
        b = qq[:, width + hd * HEAD_PAD:width + (hd + 1) * HEAD_PAD]
        q_ref[0, hd] = (a * cq_t + b * sq_t).astype(BF16)


def _proj_ctx_kernel(ctx_ref, mod_ref, watt_ref, gkv_ref, wk_ref, wvt_ref, k_ref, vt_ref):
    mod = mod_ref[0]
    ctx = ctx_ref[...].reshape(-1, ctx_ref.shape[2])
    h = ctx * (1.0 + mod[1:2]) + mod[0:1]
    p = _dot_nt(h, watt_ref[Q_LORA:Q_LORA + KV_LORA + LANES, :])
    _keys_values(p[:, 0:KV_LORA], p[:, KV_LORA:], gkv_ref, wk_ref, wvt_ref, k_ref, vt_ref)


def _proj_x(x, mods, tab, watt, gq, gkv, wq, wk, wvt):
    bsz, n, d = x.shape
    assert n % PROJ_TILE == 0
    const = lambda j, b: (0, 0)
    one = pl.Buffered(1)
    head_block = pl.BlockSpec((1, MLA_HEADS, PROJ_TILE, HEAD_PAD), lambda j, b: (b, 0, j, 0))
    return pl.pallas_call(
        _proj_x_kernel,
        grid=(n // PROJ_TILE, bsz),
        in_specs=[
            pl.BlockSpec((1, PROJ_TILE, d), lambda j, b: (b, j, 0)),
            pl.BlockSpec((1, SUBLANES, d), lambda j, b: (b, 0, 0)),
            pl.BlockSpec((PROJ_TILE, 4 * LANES), lambda j, b: (j, 0)),
            pl.BlockSpec(watt.shape, const, pipeline_mode=one),
            pl.BlockSpec(gq.shape, const, pipeline_mode=one),
            pl.BlockSpec(gkv.shape, const, pipeline_mode=one),
            pl.BlockSpec(wq.shape, const, pipeline_mode=one),
            pl.BlockSpec(wk.shape, const, pipeline_mode=one),
            pl.BlockSpec(wvt.shape, const, pipeline_mode=one),
        ],
        out_specs=[head_block, head_block,
                   pl.BlockSpec((1, MLA_HEADS, V_ROWS, PROJ_TILE), lambda j, b: (b, 0, 0, j))],
        out_shape=[
            jax.ShapeDtypeStruct((bsz, MLA_HEADS, n, HEAD_PAD), BF16),
            jax.ShapeDtypeStruct((bsz, MLA_HEADS, n, HEAD_PAD), BF16),
            jax.ShapeDtypeStruct((bsz, MLA_HEADS, V_ROWS, n), BF16),
        ],
        compiler_params=pltpu.CompilerParams(
            dimension_semantics=("arbitrary", "arbitrary"), vmem_limit_bytes=VMEM_LIMIT),
        name="proj_x",
    )(x, mods, tab, watt, gq, gkv, wq, wk, wvt)


def _proj_ctx(ctx, mods, watt, gkv, wk, wvt):
    bsz, n_ctx, d = ctx.shape
    assert bsz % CTX_BATCH == 0
    const = lambda b: (0, 0)
    one = pl.Buffered(1)
    return pl.pallas_call(
        _proj_ctx_kernel,
        grid=(bsz // CTX_BATCH,),
        in_specs=[
            pl.BlockSpec((CTX_BATCH, n_ctx, d), lambda b: (b, 0, 0)),
            pl.BlockSpec((1, SUBLANES, d), lambda b: (bsz, 0, 0)),
            pl.BlockSpec(watt.shape, const, pipeline_mode=one),
            pl.BlockSpec(gkv.shape, const, pipeline_mode=one),
            pl.BlockSpec(wk.shape, const, pipeline_mode=one),
            pl.BlockSpec(wvt.shape, const, pipeline_mode=one),
        ],
        out_specs=[pl.BlockSpec((CTX_BATCH, MLA_HEADS, n_ctx, HEAD_PAD), lambda b: (b, 0, 0, 0)),
                   pl.BlockSpec((CTX_BATCH, MLA_HEADS, V_ROWS, n_ctx), lambda b: (b, 0, 0, 0))],
        out_shape=[
            jax.ShapeDtypeStruct((bsz, MLA_HEADS, n_ctx, HEAD_PAD), BF16),
            jax.ShapeDtypeStruct((bsz, MLA_HEADS, V_ROWS, n_ctx), BF16),
        ],
        compiler_params=pltpu.CompilerParams(dimension_semantics=("arbitrary",), vmem_limit_bytes=VMEM_LIMIT),
        name="proj_ctx",
    )(ctx, mods, watt, gkv, wk, wvt)


def _attn_kernel(q_ref, kx_ref, kc_ref, vtx_ref, vtc_ref, o_ref, s_ref):
    heads = q_ref.shape[1]
    nq = q_ref.shape[2] // Q_TILE
    nx = kx_ref.shape[2]

    def scores(h, i, slot):
        q = q_ref[0, h, i * Q_TILE:(i + 1) * Q_TILE, :]
        s_ref[slot, 0:nx] = _dot_nt(kx_ref[0, h], q)
        s_ref[slot, nx:] = _dot_nt(kc_ref[0, h], q)

    def finish(h, i, slot):
        s_t = s_ref[slot]
        m = jnp.max(s_t, axis=0, keepdims=True)
        p_t = jnp.exp2(s_t - m).astype(BF16)
        o_t = _dot(vtx_ref[0, h], p_t[0:nx]) + _dot(vtc_ref[0, h], p_t[nx:])
        o_ref[0, h, i] = o_t[0:V_DIM] * (1.0 / o_t[V_DIM:V_DIM + 1])

    def head(h, next_h):
        for i in range(nq):
            slot = i % 2
            if i + 1 < nq:
                scores(h, i + 1, 1 - slot)
            elif next_h is not None:
                scores(next_h, 0, 1 - slot)
            finish(h, i, slot)

    assert nq % 2 == 0
    scores(0, 0, 0)

    def body(h, carry):
        head(h, h + 1)
        return carry

    lax.fori_loop(0, heads - 1, body, 0)
    head(heads - 1, None)


def _attn(q, kx, kc, vtx, vtc):
    bsz, heads, n, _ = q.shape
    n_ctx = kc.shape[2]
    n_keys = n + n_ctx
    nq = n // Q_TILE
    whole = lambda b: (b, 0, 0, 0)
    return pl.pallas_call(
        _attn_kernel,
        grid=(bsz,),
        in_specs=[
            pl.BlockSpec((1, heads, n, HEAD_PAD), whole),
            pl.BlockSpec((1, heads, n, HEAD_PAD), whole),
            pl.BlockSpec((1, heads, n_ctx, HEAD_PAD), whole),
            pl.BlockSpec((1, heads, V_ROWS, n), whole),
            pl.BlockSpec((1, heads, V_ROWS, n_ctx), whole),
        ],
        out_specs=pl.BlockSpec((1, heads, nq, V_DIM, Q_TILE), lambda b: (b, 0, 0, 0, 0)),
        out_shape=jax.ShapeDtypeStruct((bsz, heads, nq, V_DIM, Q_TILE), F32),
        scratch_shapes=[pltpu.VMEM((2, n_keys, Q_TILE), F32)],
        compiler_params=pltpu.CompilerParams(
            dimension_semantics=("arbitrary",), vmem_limit_bytes=VMEM_LIMIT),
        name="attn",
    )(q, kx, kc, vtx, vtc)


def _mixer_kernel(x_ref, xp_ref, xn_ref, mod_ref, att_ref, wt_ref, cw_ref, woc_ref, wom_ref, wo_ref, ln_ref,
                  o_ref, res_ref, *, tiles_per_seq):
    t = pl.program_id(0)
    n_tiles = pl.num_programs(0) - 1

    def layer_norm():
        res = res_ref[...]
        mu = jnp.mean(res, axis=-1, keepdims=True)
        dev = res - mu
        var = jnp.mean(dev * dev, axis=-1, keepdims=True)
        ln = ln_ref[...]
        o_ref[0] = dev * lax.rsqrt(var + LN_EPS) * ln[0:1] + ln[1:2]

    @pl.when(t == 0)
    def _():
        res_ref[...] = jnp.zeros_like(res_ref)

    @pl.when(t == n_tiles)
    def _():
        layer_norm()

    @pl.when(t < n_tiles)
    def _():
        layer_norm()
        j = t % tiles_per_seq
        d = x_ref.shape[2]
        x = x_ref[0]
        rows = x.shape[0]
        mod = mod_ref[0]
        shift = mod[0:1]
        scale1 = 1.0 + mod[1:2]
        gate = mod[2:3]
        hx = x * scale1 + shift

        def proj(h, off, width):
            return _dot_nt(h, wt_ref[off:off + width, :])

        xh = jnp.concatenate([xp_ref[0], xn_ref[0]], axis=0)
        hh = xh * scale1 + shift
        hx_ext = jnp.concatenate([hx, hh], axis=0)
        u_ext = proj(hx_ext, O_XC, D_CONV) * proj(hx_ext, O_CC, D_CONV)
        u, uh = u_ext[:rows], u_ext[rows:]
        prev_edge = uh[SUBLANES - 1:SUBLANES] * (j > 0).astype(F32)
        next_edge = uh[SUBLANES:SUBLANES + 1] * (j < tiles_per_seq - 1).astype(F32)

        s_conv = jax.nn.sigmoid(proj(hx, O_GCONV, d))
        s_mla = jax.nn.sigmoid(proj(hx, O_GCONV + d, d))
        gm = _silu(proj(hx, O_GM, D_MLA))

        r = lax.broadcasted_iota(jnp.int32, u.shape, 0)
        u_prev = jnp.where(r == 0, prev_edge, pltpu.roll(u, 1, 0))
        u_next = jnp.where(r == rows - 1, next_edge, pltpu.roll(u, rows - 1, 0))
        cw = cw_ref[...]
        conv = u_prev * cw[0:1] + u * cw[1:2] + u_next * cw[2:3]

        zc = _silu(proj(hx, O_GC, D_CONV)) * proj(hx, O_BC, D_CONV) * conv
        yc = _dot(zc, woc_ref[...])

        att = att_ref[0, :, 0].reshape(D_MLA, rows).T
        ym = _dot(gm * att, wom_ref[...])

        merged = s_conv * yc + s_mla * ym
        y = _dot(merged, wo_ref[...])
        res_ref[...] = DEEPNORM_ALPHA * x + gate * y


def _mixer(x, mods, att_t, wt, cw, woc, wom, wo, ln):
    bsz, n, d = x.shape
    assert MIX_TILE == Q_TILE and n % MIX_TILE == 0
    nt = n // MIX_TILE
    n_tiles = bsz * nt
    blocks8 = MIX_TILE // SUBLANES
    const = lambda t: (0, 0)
    one = pl.Buffered(1)

    def tile(t):
        t = jnp.minimum(t, n_tiles - 1)
        return t // nt, t % nt

    def x_map(t):
        b, j = tile(t)
        return b, j, 0

    def prev_map(t):
        b, j = tile(t)
        return b, jnp.maximum(j * blocks8 - 1, 0), 0

    def next_map(t):
        b, j = tile(t)
        return b, jnp.minimum((j + 1) * blocks8, n // SUBLANES - 1), 0

    def mod_map(t):
        return tile(t)[0], 0, 0

    def att_map(t):
        b, j = tile(t)
        return b, 0, j, 0, 0

    def out_map(t):
        t = jnp.maximum(t - 1, 0)
        return t // nt, t % nt, 0

    return pl.pallas_call(
        functools.partial(_mixer_kernel, tiles_per_seq=nt),
        grid=(n_tiles + 1,),
        in_specs=[
            pl.BlockSpec((1, MIX_TILE, d), x_map),
            pl.BlockSpec((1, SUBLANES, d), prev_map),
            pl.BlockSpec((1, SUBLANES, d), next_map),
            pl.BlockSpec((1, SUBLANES, d), mod_map),
            pl.BlockSpec((1, MLA_HEADS, 1, V_DIM, MIX_TILE), att_map),
            pl.BlockSpec(wt.shape, const, pipeline_mode=one),
            pl.BlockSpec(cw.shape, const, pipeline_mode=one),
            pl.BlockSpec(woc.shape, const, pipeline_mode=one),
            pl.BlockSpec(wom.shape, const, pipeline_mode=one),
            pl.BlockSpec(wo.shape, const, pipeline_mode=one),
            pl.BlockSpec(ln.shape, const, pipeline_mode=one),
        ],
        out_specs=pl.BlockSpec((1, MIX_TILE, d), out_map),
        out_shape=jax.ShapeDtypeStruct((bsz, n, d), F32),
        scratch_shapes=[pltpu.VMEM((MIX_TILE, d), F32)],
        compiler_params=pltpu.CompilerParams(dimension_semantics=("arbitrary",), vmem_limit_bytes=VMEM_LIMIT),
        name="mixer",
    )(x, x, x, mods, att_t, wt, cw, woc, wom, wo, ln)


def _rope_pair_swap():
    q = QK_ROPE // 4
    idx = np.concatenate([np.arange(q, 2 * q), np.arange(0, q), np.arange(3 * q, 4 * q), np.arange(2 * q, 3 * q)])
    sign = np.concatenate([-np.ones(q), np.ones(q), -np.ones(q), np.ones(q)]).astype(np.float32)
    return idx, sign


def _rope_tables(n, scale):
    rows = n // GRID_W
    row_pos = np.repeat(np.arange(rows), GRID_W).astype(np.float64)
    col_pos = np.tile(np.arange(GRID_W), rows).astype(np.float64)
    axis_dim = QK_ROPE // 2
    inv_freq = ROPE_THETA ** (-np.arange(0, axis_dim, 2, dtype=np.float64) / axis_dim)
    ang_r = row_pos[:, None] * inv_freq[None, :]
    ang_c = col_pos[:, None] * inv_freq[None, :]
    cos = np.concatenate([np.cos(ang_r), np.cos(ang_r), np.cos(ang_c), np.cos(ang_c)], axis=-1)
    sin = np.concatenate([np.sin(ang_r), np.sin(ang_r), np.sin(ang_c), np.sin(ang_c)], axis=-1)
    pad = np.zeros((n, HEAD_PAD - QK_NOPE - QK_ROPE))
    zero_n = np.zeros((n, QK_NOPE))
    cq = np.concatenate([np.full((n, QK_NOPE), scale), cos * scale, pad], axis=-1)
    sq = np.concatenate([zero_n, sin * scale, pad], axis=-1)
    ck = np.concatenate([zero_n, cos, pad], axis=-1)
    sk = np.concatenate([zero_n, sin, pad], axis=-1)
    return jnp.asarray(np.concatenate([cq, sq, ck, sk], axis=-1), dtype=F32)


def _place_rope_rows(w):
    d = w.shape[1]
    return jnp.concatenate([jnp.zeros((QK_NOPE, d), w.dtype), w,
                            jnp.zeros((HEAD_PAD - QK_NOPE - QK_ROPE, d), w.dtype)], axis=0)


def kernel(x, c, ctx, c_ctx, w_ada, b_ada, w_in, conv_w, q_norm_g, w_uq, kv_norm_g, w_ukv, w_out_conv,
           w_out_mla, w_o, ln_g, ln_b):
    bsz, n, d = x.shape
    assert w_ada.shape[0] == DEPTH == 1
    w_ada, b_ada, w_in, conv_w = w_ada[0], b_ada[0], w_in[0], conv_w[0]
    q_norm_g, w_uq, kv_norm_g, w_ukv = q_norm_g[0], w_uq[0], kv_norm_g[0], w_ukv[0]
    w_out_conv, w_out_mla, w_o, ln_g, ln_b = w_out_conv[0], w_out_mla[0], w_o[0], ln_g[0], ln_b[0]

    pad_rows = (-(bsz + 1)) % SUBLANES
    c_rows = jnp.concatenate([c, c_ctx[None, :], jnp.zeros((pad_rows, d), F32)], axis=0)
    mod = _adaln(c_rows, w_ada, b_ada[None, :])
    mods = mod[:bsz + 1].reshape(bsz + 1, 3, d)
    mods = jnp.concatenate([mods, jnp.zeros((bsz + 1, SUBLANES - 3, d), F32)], axis=1)

    wt = jnp.swapaxes(w_in, 0, 1)

    idx, sign = _rope_pair_swap()
    w_kr = wt[O_KR:O_KR + QK_ROPE]
    w_kr_rot = w_kr[idx] * sign[:, None]
    watt = jnp.concatenate([wt[O_CQ:O_KR], _place_rope_rows(w_kr), _place_rope_rows(w_kr_rot)], axis=0)

    hq = QK_NOPE + QK_ROPE
    w_uq_h = w_uq.reshape(Q_LORA, MLA_HEADS, hq)
    zq = jnp.zeros((Q_LORA, MLA_HEADS, HEAD_PAD - hq), F32)
    wq_plain = jnp.concatenate([w_uq_h, zq], axis=-1).reshape(Q_LORA, MLA_HEADS * HEAD_PAD)
    wq_rot = jnp.concatenate([jnp.zeros((Q_LORA, MLA_HEADS, QK_NOPE), F32),
                              w_uq_h[..., QK_NOPE + idx] * sign, zq], axis=-1).reshape(Q_LORA, MLA_HEADS * HEAD_PAD)
    wq = jnp.concatenate([wq_plain, wq_rot], axis=-1)

    w_ukv_h = w_ukv.reshape(KV_LORA, MLA_HEADS, QK_NOPE + V_DIM)
    wk = jnp.concatenate([w_ukv_h[..., :QK_NOPE], jnp.zeros((KV_LORA, MLA_HEADS, HEAD_PAD - QK_NOPE), F32)],
                         axis=-1).reshape(KV_LORA, MLA_HEADS * HEAD_PAD)
    wv = jnp.concatenate([w_ukv_h[..., QK_NOPE:], jnp.zeros((KV_LORA, MLA_HEADS, V_ROWS - V_DIM), F32)], axis=-1)
    wvt = wv.reshape(KV_LORA, MLA_HEADS * V_ROWS).T

    tab = _rope_tables(n, float((QK_NOPE + QK_ROPE) ** -0.5 * np.log2(np.e)))
    gq, gkv = q_norm_g[None, :], kv_norm_g[None, :]
    q, kx, vtx = _proj_x(x, mods, tab, watt, gq, gkv, wq, wk, wvt)
    kc, vtc = _proj_ctx(ctx, mods, watt, gkv, wk, wvt)
    att_t = _attn(q, kx, kc, vtx, vtc)

    cw = jnp.concatenate([conv_w, jnp.zeros((SUBLANES - conv_w.shape[0], D_CONV), F32)], axis=0)
    ln = jnp.concatenate([ln_g[None, :], ln_b[None, :], jnp.zeros((SUBLANES - 2, d), F32)], axis=0)
    return _mixer(x, mods, att_t, wt, cw, w_out_conv, w_out_mla, w_o, ln)
```

```python
import functools

import jax
import jax.numpy as jnp
import numpy as np
from jax import lax
from jax.experimental import pallas as pl
from jax.experimental.pallas import tpu as pltpu

GRID_W = 64
D_CONV = 512
MLA_HEADS = 8
QK_NOPE = 64
QK_ROPE = 32
V_DIM = 64
D_MLA = MLA_HEADS * V_DIM
Q_LORA = 256
KV_LORA = 128
ROPE_THETA = 10000.0
LN_EPS = 1e-5
RMS_EPS = 1e-6
DEPTH = 1
DEEPNORM_ALPHA = (2.0 * DEPTH) ** 0.25

O_XC, O_BC, O_CC, O_GC = 0, D_CONV, 2 * D_CONV, 3 * D_CONV
O_CQ = 4 * D_CONV
O_CKV = O_CQ + Q_LORA
O_KR = O_CKV + KV_LORA
O_GM = O_KR + QK_ROPE
O_GCONV = O_GM + D_MLA

LANES = 128
SUBLANES = 8
HEAD_PAD = LANES
V_ROWS = V_DIM + 16
PROJ_TILE = 1024
CTX_BATCH = 4
MIX_TILE = 512
Q_TILE = 512
VMEM_LIMIT = 56 * 1024 * 1024

F32 = jnp.float32
BF16 = jnp.bfloat16


def _dot(a, b):
    return jnp.dot(a, b, preferred_element_type=F32)


def _dot_nt(a, b):
    return lax.dot_general(a, b, (((1,), (1,)), ((), ())), preferred_element_type=F32)


def _silu(t):
    return t * jax.nn.sigmoid(t)


def _rms_norm(t, g):
    return t * lax.rsqrt(jnp.mean(t * t, axis=-1, keepdims=True) + RMS_EPS) * g


def _adaln_kernel(c_ref, w_ref, b_ref, o_ref):
    cs = _silu(c_ref[...])
    o_ref[...] = _dot(cs, w_ref[...]) + b_ref[...]


def _adaln(c_rows, w_ada, b_ada):
    rows, d = c_rows.shape
    n_out = w_ada.shape[1]
    col_tile = 1024
    return pl.pallas_call(
        _adaln_kernel,
        grid=(n_out // col_tile,),
        in_specs=[
            pl.BlockSpec((rows, d), lambda i: (0, 0)),
            pl.BlockSpec((d, col_tile), lambda i: (0, i)),
            pl.BlockSpec((1, col_tile), lambda i: (0, i)),
        ],
        out_specs=pl.BlockSpec((rows, col_tile), lambda i: (0, i)),
        out_shape=jax.ShapeDtypeStruct((rows, n_out), F32),
        name="adaln",
    )(c_rows, w_ada, b_ada)


def _keys_values(ckv, kr_placed, gkv_ref, wk_ref, wvt_ref, k_ref, vt_ref):
    groups, per = k_ref.shape[0], k_ref.shape[2]
    kvn = _rms_norm(ckv, gkv_ref[...])
    kn = _dot(kvn, wk_ref[...])
    vt = _dot_nt(wvt_ref[...], kvn)
    r = lax.broadcasted_iota(jnp.int32, vt.shape, 0)
    ones = r == V_DIM
    for h in range(1, MLA_HEADS):
        ones = ones | (r == h * V_ROWS + V_DIM)
    vt = jnp.where(ones, 1.0, vt).astype(BF16)
    for g in range(groups):
        rows = slice(g * per, (g + 1) * per)
        for h in range(MLA_HEADS):
            k_ref[g, h] = (kn[rows, h * HEAD_PAD:(h + 1) * HEAD_PAD] + kr_placed[rows]).astype(BF16)
        vt_ref[g] = vt[:, rows].reshape(MLA_HEADS, V_ROWS, per)


def _proj_x_kernel(x_ref, mod_ref, tab_ref, watt_ref, gq_ref, gkv_ref, wq_ref, wk_ref, wvt_ref,
                   q_ref, k_ref, vt_ref):
    mod = mod_ref[0]
    h = x_ref[0] * (1.0 + mod[1:2]) + mod[0:1]
    p = _dot_nt(h, watt_ref[...])
    tab = tab_ref[...]
    cq_t, sq_t, ck_t = tab[:, 0:LANES], tab[:, LANES:2 * LANES], tab[:, 2 * LANES:3 * LANES]
    sk_up, sk_down = tab[:, 3 * LANES:4 * LANES], tab[:, 4 * LANES:5 * LANES]
    kr0 = Q_LORA + KV_LORA
    kr = p[:, kr0:kr0 + LANES]
    quarter = QK_ROPE // 4
    kr_placed = (kr * ck_t + pltpu.roll(kr, quarter, 1) * sk_up
                 + pltpu.roll(kr, LANES - quarter, 1) * sk_down)
    _keys_values(p[:, Q_LORA:kr0], kr_placed, gkv_ref, wk_ref, wvt_ref, k_ref, vt_ref)
    qn = _rms_norm(p[:, 0:Q_LORA], gq_ref[...])
    qq = _dot(qn, wq_ref[...])
    width = MLA_HEADS * HEAD_PAD
    for hd in range(MLA_HEADS):
        a = qq[:, hd * HEAD_PAD:(hd + 1) * HEAD_PAD]
        b = qq[:, width + hd * HEAD_PAD:width + (hd + 1) * HEAD_PAD]
        q_ref[0, hd] = (a * cq_t + b * sq_t).astype(BF16)


def _proj_ctx_kernel(ctx_ref, mod_ref, watt_ref, gkv_ref, wk_ref, wvt_ref, k_ref, vt_ref):
    mod = mod_ref[0]
    ctx = ctx_ref[...].reshape(-1, ctx_ref.shape[2])
    h = ctx * (1.0 + mod[1:2]) + mod[0:1]
    p = _dot_nt(h, watt_ref[Q_LORA:Q_LORA + KV_LORA + LANES, :])
    _keys_values(p[:, 0:KV_LORA], p[:, KV_LORA:], gkv_ref, wk_ref, wvt_ref, k_ref, vt_ref)


def _proj_x(x, mods, tab, watt, gq, gkv, wq, wk, wvt):
    bsz, n, d = x.shape
    assert n % PROJ_TILE == 0
    const = lambda j, b: (0, 0)
    one = pl.Buffered(1)
    head_block = pl.BlockSpec((1, MLA_HEADS, PROJ_TILE, HEAD_PAD), lambda j, b: (b, 0, j, 0))
    return pl.pallas_call(
        _proj_x_kernel,
        grid=(n // PROJ_TILE, bsz),
        in_specs=[
            pl.BlockSpec((1, PROJ_TILE, d), lambda j, b: (b, j, 0)),
            pl.BlockSpec((1, SUBLANES, d), lambda j, b: (b, 0, 0)),
            pl.BlockSpec((PROJ_TILE, tab.shape[1]), lambda j, b: (j, 0)),
            pl.BlockSpec(watt.shape, const, pipeline_mode=one),
            pl.BlockSpec(gq.shape, const, pipeline_mode=one),
            pl.BlockSpec(gkv.shape, const, pipeline_mode=one),
            pl.BlockSpec(wq.shape, const, pipeline_mode=one),
            pl.BlockSpec(wk.shape, const, pipeline_mode=one),
            pl.BlockSpec(wvt.shape, const, pipeline_mode=one),
        ],
        out_specs=[head_block, head_block,
                   pl.BlockSpec((1, MLA_HEADS, V_ROWS, PROJ_TILE), lambda j, b: (b, 0, 0, j))],
        out_shape=[
            jax.ShapeDtypeStruct((bsz, MLA_HEADS, n, HEAD_PAD), BF16),
            jax.ShapeDtypeStruct((bsz, MLA_HEADS, n, HEAD_PAD), BF16),
            jax.ShapeDtypeStruct((bsz, MLA_HEADS, V_ROWS, n), BF16),
        ],
        compiler_params=pltpu.CompilerParams(
            dimension_semantics=("arbitrary", "arbitrary"), vmem_limit_bytes=VMEM_LIMIT),
        name="proj_x",
    )(x, mods, tab, watt, gq, gkv, wq, wk, wvt)


def _proj_ctx(ctx, mods, watt, gkv, wk, wvt):
    bsz, n_ctx, d = ctx.shape
    assert bsz % CTX_BATCH == 0
    const = lambda b: (0, 0)
    one = pl.Buffered(1)
    return pl.pallas_call(
        _proj_ctx_kernel,
        grid=(bsz // CTX_BATCH,),
        in_specs=[
            pl.BlockSpec((CTX_BATCH, n_ctx, d), lambda b: (b, 0, 0)),
            pl.BlockSpec((1, SUBLANES, d), lambda b: (bsz, 0, 0)),
            pl.BlockSpec(watt.shape, const, pipeline_mode=one),
            pl.BlockSpec(gkv.shape, const, pipeline_mode=one),
            pl.BlockSpec(wk.shape, const, pipeline_mode=one),
            pl.BlockSpec(wvt.shape, const, pipeline_mode=one),
        ],
        out_specs=[pl.BlockSpec((CTX_BATCH, MLA_HEADS, n_ctx, HEAD_PAD), lambda b: (b, 0, 0, 0)),
                   pl.BlockSpec((CTX_BATCH, MLA_HEADS, V_ROWS, n_ctx), lambda b: (b, 0, 0, 0))],
        out_shape=[
            jax.ShapeDtypeStruct((bsz, MLA_HEADS, n_ctx, HEAD_PAD), BF16),
            jax.ShapeDtypeStruct((bsz, MLA_HEADS, V_ROWS, n_ctx), BF16),
        ],
        compiler_params=pltpu.CompilerParams(dimension_semantics=("arbitrary",), vmem_limit_bytes=VMEM_LIMIT),
        name="proj_ctx",
    )(ctx, mods, watt, gkv, wk, wvt)


def _attn_kernel(q_ref, kx_ref, kc_ref, vtx_ref, vtc_ref, o_ref, s_ref):
    heads = q_ref.shape[1]
    nq = q_ref.shape[2] // Q_TILE
    nx = kx_ref.shape[2]

    def scores(h, i, slot):
        q = q_ref[0, h, i * Q_TILE:(i + 1) * Q_TILE, :]
        s_ref[slot, 0:nx] = _dot_nt(kx_ref[0, h], q)
        s_ref[slot, nx:] = _dot_nt(kc_ref[0, h], q)

    def finish(h, i, slot):
        s_t = s_ref[slot]
        m = jnp.max(s_t, axis=0, keepdims=True)
        p_t = jnp.exp2(s_t - m).astype(BF16)
        o_t = _dot(vtx_ref[0, h], p_t[0:nx]) + _dot(vtc_ref[0, h], p_t[nx:])
        o_ref[0, h, i] = o_t[0:V_DIM] * (1.0 / o_t[V_DIM:V_DIM + 1])

    def head(h, next_h):
        for i in range(nq):
            slot = i % 2
            if i + 1 < nq:
                scores(h, i + 1, 1 - slot)
            elif next_h is not None:
                scores(next_h, 0, 1 - slot)
            finish(h, i, slot)

    assert nq % 2 == 0
    scores(0, 0, 0)

    def body(h, carry):
        head(h, h + 1)
        return carry

    lax.fori_loop(0, heads - 1, body, 0)
    head(heads - 1, None)


def _attn(q, kx, kc, vtx, vtc):
    bsz, heads, n, _ = q.shape
    n_ctx = kc.shape[2]
    n_keys = n + n_ctx
    nq = n // Q_TILE
    whole = lambda b: (b, 0, 0, 0)
    return pl.pallas_call(
        _attn_kernel,
        grid=(bsz,),
        in_specs=[
            pl.BlockSpec((1, heads, n, HEAD_PAD), whole),
            pl.BlockSpec((1, heads, n, HEAD_PAD), whole),
            pl.BlockSpec((1, heads, n_ctx, HEAD_PAD), whole),
            pl.BlockSpec((1, heads, V_ROWS, n), whole),
            pl.BlockSpec((1, heads, V_ROWS, n_ctx), whole),
        ],
        out_specs=pl.BlockSpec((1, heads, nq, V_DIM, Q_TILE), lambda b: (b, 0, 0, 0, 0)),
        out_shape=jax.ShapeDtypeStruct((bsz, heads, nq, V_DIM, Q_TILE), F32),
        scratch_shapes=[pltpu.VMEM((2, n_keys, Q_TILE), F32)],
        compiler_params=pltpu.CompilerParams(
            dimension_semantics=("arbitrary",), vmem_limit_bytes=VMEM_LIMIT),
        name="attn",
    )(q, kx, kc, vtx, vtc)


def _mixer_kernel(x_ref, xp_ref, xn_ref, mod_ref, att_ref, wt_ref, cw_ref, woc_ref, wom_ref, wo_ref, ln_ref,
                  o_ref, res_ref, *, tiles_per_seq):
    t = pl.program_id(0)
    n_tiles = pl.num_programs(0) - 1

    def layer_norm():
        res = res_ref[...]
        mu = jnp.mean(res, axis=-1, keepdims=True)
        dev = res - mu
        var = jnp.mean(dev * dev, axis=-1, keepdims=True)
        ln = ln_ref[...]
        o_ref[0] = dev * lax.rsqrt(var + LN_EPS) * ln[0:1] + ln[1:2]

    @pl.when(t == 0)
    def _():
        res_ref[...] = jnp.zeros_like(res_ref)

    @pl.when(t == n_tiles)
    def _():
        layer_norm()

    @pl.when(t < n_tiles)
    def _():
        layer_norm()
        j = t % tiles_per_seq
        d = x_ref.shape[2]
        x = x_ref[0]
        rows = x.shape[0]
        mod = mod_ref[0]
        shift = mod[0:1]
        scale1 = 1.0 + mod[1:2]
        gate = mod[2:3]
        hx = x * scale1 + shift

        def proj(h, off, width):
            return _dot_nt(h, wt_ref[off:off + width, :])

        xh = jnp.concatenate([xp_ref[0], xn_ref[0]], axis=0)
        hh = xh * scale1 + shift
        hx_ext = jnp.concatenate([hx, hh], axis=0)
        u_ext = proj(hx_ext, O_XC, D_CONV) * proj(hx_ext, O_CC, D_CONV)
        u, uh = u_ext[:rows], u_ext[rows:]
        prev_edge = uh[SUBLANES - 1:SUBLANES] * (j > 0).astype(F32)
        next_edge = uh[SUBLANES:SUBLANES + 1] * (j < tiles_per_seq - 1).astype(F32)

        s_conv = jax.nn.sigmoid(proj(hx, O_GCONV, d))
        s_mla = jax.nn.sigmoid(proj(hx, O_GCONV + d, d))
        gm = _silu(proj(hx, O_GM, D_MLA))

        r = lax.broadcasted_iota(jnp.int32, u.shape, 0)
        u_prev = jnp.where(r == 0, prev_edge, pltpu.roll(u, 1, 0))
        u_next = jnp.where(r == rows - 1, next_edge, pltpu.roll(u, rows - 1, 0))
        cw = cw_ref[...]
        conv = u_prev * cw[0:1] + u * cw[1:2] + u_next * cw[2:3]

        zc = _silu(proj(hx, O_GC, D_CONV)) * proj(hx, O_BC, D_CONV) * conv
        yc = _dot(zc, woc_ref[...])

        att = att_ref[0, :, 0].reshape(D_MLA, rows).T
        ym = _dot(gm * att, wom_ref[...])

        merged = s_conv * yc + s_mla * ym
        y = _dot(merged, wo_ref[...])
        res_ref[...] = DEEPNORM_ALPHA * x + gate * y


def _mixer(x, mods, att_t, wt, cw, woc, wom, wo, ln):
    bsz, n, d = x.shape
    assert MIX_TILE == Q_TILE and n % MIX_TILE == 0
    nt = n // MIX_TILE
    n_tiles = bsz * nt
    blocks8 = MIX_TILE // SUBLANES
    const = lambda t: (0, 0)
    one = pl.Buffered(1)

    def tile(t):
        t = jnp.minimum(t, n_tiles - 1)
        return t // nt, t % nt

    def x_map(t):
        b, j = tile(t)
        return b, j, 0

    def prev_map(t):
        b, j = tile(t)
        return b, jnp.maximum(j * blocks8 - 1, 0), 0

    def next_map(t):
        b, j = tile(t)
        return b, jnp.minimum((j + 1) * blocks8, n // SUBLANES - 1), 0

    def mod_map(t):
        return tile(t)[0], 0, 0

    def att_map(t):
        b, j = tile(t)
        return b, 0, j, 0, 0

    def out_map(t):
        t = jnp.maximum(t - 1, 0)
        return t // nt, t % nt, 0

    return pl.pallas_call(
        functools.partial(_mixer_kernel, tiles_per_seq=nt),
        grid=(n_tiles + 1,),
        in_specs=[
            pl.BlockSpec((1, MIX_TILE, d), x_map),
            pl.BlockSpec((1, SUBLANES, d), prev_map),
            pl.BlockSpec((1, SUBLANES, d), next_map),
            pl.BlockSpec((1, SUBLANES, d), mod_map),
            pl.BlockSpec((1, MLA_HEADS, 1, V_DIM, MIX_TILE), att_map),
            pl.BlockSpec(wt.shape, const, pipeline_mode=one),
            pl.BlockSpec(cw.shape, const, pipeline_mode=one),
            pl.BlockSpec(woc.shape, const, pipeline_mode=one),
            pl.BlockSpec(wom.shape, const, pipeline_mode=one),
            pl.BlockSpec(wo.shape, const, pipeline_mode=one),
            pl.BlockSpec(ln.shape, const, pipeline_mode=one),
        ],
        out_specs=pl.BlockSpec((1, MIX_TILE, d), out_map),
        out_shape=jax.ShapeDtypeStruct((bsz, n, d), F32),
        scratch_shapes=[pltpu.VMEM((MIX_TILE, d), F32)],
        compiler_params=pltpu.CompilerParams(dimension_semantics=("arbitrary",), vmem_limit_bytes=VMEM_LIMIT),
        name="mixer",
    )(x, x, x, mods, att_t, wt, cw, woc, wom, wo, ln)


def _rope_pair_swap():
    q = QK_ROPE // 4
    idx = np.concatenate([np.arange(q, 2 * q), np.arange(0, q), np.arange(3 * q, 4 * q), np.arange(2 * q, 3 * q)])
    sign = np.concatenate([-np.ones(q), np.ones(q), -np.ones(q), np.ones(q)]).astype(np.float32)
    return idx, sign


def _rope_tables(n, scale):
    rows = n // GRID_W
    row_pos = np.repeat(np.arange(rows), GRID_W).astype(np.float64)
    col_pos = np.tile(np.arange(GRID_W), rows).astype(np.float64)
    axis_dim = QK_ROPE // 2
    inv_freq = ROPE_THETA ** (-np.arange(0, axis_dim, 2, dtype=np.float64) / axis_dim)
    ang_r = row_pos[:, None] * inv_freq[None, :]
    ang_c = col_pos[:, None] * inv_freq[None, :]
    cos = np.concatenate([np.cos(ang_r), np.cos(ang_r), np.cos(ang_c), np.cos(ang_c)], axis=-1)
    sin = np.concatenate([np.sin(ang_r), np.sin(ang_r), np.sin(ang_c), np.sin(ang_c)], axis=-1)
    pad = np.zeros((n, HEAD_PAD - QK_NOPE - QK_ROPE))
    zero_n = np.zeros((n, QK_NOPE))
    cq = np.concatenate([np.full((n, QK_NOPE), scale), cos * scale, pad], axis=-1)
    sq = np.concatenate([zero_n, sin * scale, pad], axis=-1)
    ck = np.concatenate([zero_n, cos, pad], axis=-1)
    quarter = QK_ROPE // 4
    upper = (np.arange(QK_ROPE) // quarter) % 2 == 1
    sk_up = np.concatenate([zero_n, np.where(upper, sin, 0.0), pad], axis=-1)
    sk_down = np.concatenate([zero_n, np.where(upper, 0.0, -sin), pad], axis=-1)
    return jnp.asarray(np.concatenate([cq, sq, ck, sk_up, sk_down], axis=-1), dtype=F32)


def _place_rope_rows(w):
    d = w.shape[1]
    return jnp.concatenate([jnp.zeros((QK_NOPE, d), w.dtype), w,
                            jnp.zeros((HEAD_PAD - QK_NOPE - QK_ROPE, d), w.dtype)], axis=0)


def kernel(x, c, ctx, c_ctx, w_ada, b_ada, w_in, conv_w, q_norm_g, w_uq, kv_norm_g, w_ukv, w_out_conv,
           w_out_mla, w_o, ln_g, ln_b):
    bsz, n, d = x.shape
    assert w_ada.shape[0] == DEPTH == 1
    w_ada, b_ada, w_in, conv_w = w_ada[0], b_ada[0], w_in[0], conv_w[0]
    q_norm_g, w_uq, kv_norm_g, w_ukv = q_norm_g[0], w_uq[0], kv_norm_g[0], w_ukv[0]
    w_out_conv, w_out_mla, w_o, ln_g, ln_b = w_out_conv[0], w_out_mla[0], w_o[0], ln_g[0], ln_b[0]

    pad_rows = (-(bsz + 1)) % SUBLANES
    c_rows = jnp.concatenate([c, c_ctx[None, :], jnp.zeros((pad_rows, d), F32)], axis=0)
    mod = _adaln(c_rows, w_ada, b_ada[None, :])
    mods = mod[:bsz + 1].reshape(bsz + 1, 3, d)
    mods = jnp.concatenate([mods, jnp.zeros((bsz + 1, SUBLANES - 3, d), F32)], axis=1)

    wt = jnp.swapaxes(w_in, 0, 1)

    idx, sign = _rope_pair_swap()
    watt = jnp.concatenate([wt[O_CQ:O_KR], _place_rope_rows(wt[O_KR:O_KR + QK_ROPE])], axis=0)

    hq = QK_NOPE + QK_ROPE
    w_uq_h = w_uq.reshape(Q_LORA, MLA_HEADS, hq)
    zq = jnp.zeros((Q_LORA, MLA_HEADS, HEAD_PAD - hq), F32)
    wq_plain = jnp.concatenate([w_uq_h, zq], axis=-1).reshape(Q_LORA, MLA_HEADS * HEAD_PAD)
    wq_rot = jnp.concatenate([jnp.zeros((Q_LORA, MLA_HEADS, QK_NOPE), F32),
                              w_uq_h[..., QK_NOPE + idx] * sign, zq], axis=-1).reshape(Q_LORA, MLA_HEADS * HEAD_PAD)
    wq = jnp.concatenate([wq_plain, wq_rot], axis=-1)

    w_ukv_h = w_ukv.reshape(KV_LORA, MLA_HEADS, QK_NOPE + V_DIM)
    wk = jnp.concatenate([w_ukv_h[..., :QK_NOPE], jnp.zeros((KV_LORA, MLA_HEADS, HEAD_PAD - QK_NOPE), F32)],
                         axis=-1).reshape(KV_LORA, MLA_HEADS * HEAD_PAD)
    wv = jnp.concatenate([w_ukv_h[..., QK_NOPE:], jnp.zeros((KV_LORA, MLA_HEADS, V_ROWS - V_DIM), F32)], axis=-1)
    wvt = wv.reshape(KV_LORA, MLA_HEADS * V_ROWS).T

    tab = _rope_tables(n, float((QK_NOPE + QK_ROPE) ** -0.5 * np.log2(np.e)))
    gq, gkv = q_norm_g[None, :], kv_norm_g[None, :]
    q, kx, vtx = _proj_x(x, mods, tab, watt, gq, gkv, wq, wk, wvt)
    kc, vtc = _proj_ctx(ctx, mods, watt, gkv, wk, wvt)
    att_t = _attn(q, kx, kc, vtx, vtc)

    cw = jnp.concatenate([conv_w, jnp.zeros((SUBLANES - conv_w.shape[0], D_CONV), F32)], axis=0)
    ln = jnp.concatenate([ln_g[None, :], ln_b[None, :], jnp.zeros((SUBLANES - 2, d), F32)], axis=0)
    return _mixer(x, mods, att_t, wt, cw, w_out_conv, w_out_mla, w_o, ln)
```

```python
import functools

import jax
import jax.numpy as jnp
import numpy as np
from jax import lax
from jax.experimental import pallas as pl
from jax.experimental.pallas import tpu as pltpu

GRID_W = 64
D_CONV = 512
MLA_HEADS = 8
QK_NOPE = 64
QK_ROPE = 32
V_DIM = 64
D_MLA = MLA_HEADS * V_DIM
Q_LORA = 256
KV_LORA = 128
ROPE_THETA = 10000.0
LN_EPS = 1e-5
RMS_EPS = 1e-6
DEPTH = 1
DEEPNORM_ALPHA = (2.0 * DEPTH) ** 0.25

O_XC, O_BC, O_CC, O_GC = 0, D_CONV, 2 * D_CONV, 3 * D_CONV
O_CQ = 4 * D_CONV
O_CKV = O_CQ + Q_LORA
O_KR = O_CKV + KV_LORA
O_GM = O_KR + QK_ROPE
O_GCONV = O_GM + D_MLA

LANES = 128
SUBLANES = 8
HEAD_PAD = LANES
V_ROWS = V_DIM + 16
PROJ_TILE = 1024
CTX_BATCH = 4
MIX_TILE = 512
Q_TILE = 256
VMEM_LIMIT = 56 * 1024 * 1024

F32 = jnp.float32
BF16 = jnp.bfloat16


def _dot(a, b):
    return jnp.dot(a, b, preferred_element_type=F32)


def _dot_nt(a, b):
    return lax.dot_general(a, b, (((1,), (1,)), ((), ())), preferred_element_type=F32)


def _silu(t):
    return t * jax.nn.sigmoid(t)


def _rms_norm(t, g):
    return t * lax.rsqrt(jnp.mean(t * t, axis=-1, keepdims=True) + RMS_EPS) * g


def _adaln_kernel(c_ref, w_ref, b_ref, o_ref):
    cs = _silu(c_ref[...])
    o_ref[...] = _dot(cs, w_ref[...]) + b_ref[...]


def _adaln(c_rows, w_ada, b_ada):
    rows, d = c_rows.shape
    n_out = w_ada.shape[1]
    col_tile = 1024
    return pl.pallas_call(
        _adaln_kernel,
        grid=(n_out // col_tile,),
        in_specs=[
            pl.BlockSpec((rows, d), lambda i: (0, 0)),
            pl.BlockSpec((d, col_tile), lambda i: (0, i)),
            pl.BlockSpec((1, col_tile), lambda i: (0, i)),
        ],
        out_specs=pl.BlockSpec((rows, col_tile), lambda i: (0, i)),
        out_shape=jax.ShapeDtypeStruct((rows, n_out), F32),
        name="adaln",
    )(c_rows, w_ada, b_ada)


def _keys_values(ckv, kr_placed, gkv_ref, wk_ref, wvt_ref, k_ref, vt_ref):
    groups, per = k_ref.shape[0], k_ref.shape[2]
    kvn = _rms_norm(ckv, gkv_ref[...])
    kn = _dot(kvn, wk_ref[...])
    vt = _dot_nt(wvt_ref[...], kvn)
    r = lax.broadcasted_iota(jnp.int32, vt.shape, 0)
    ones = r == V_DIM
    for h in range(1, MLA_HEADS):
        ones = ones | (r == h * V_ROWS + V_DIM)
    vt = jnp.where(ones, 1.0, vt).astype(BF16)
    for g in range(groups):
        rows = slice(g * per, (g + 1) * per)
        for h in range(MLA_HEADS):
            k_ref[g, h] = (kn[rows, h * HEAD_PAD:(h + 1) * HEAD_PAD] + kr_placed[rows]).astype(BF16)
        vt_ref[g] = vt[:, rows].reshape(MLA_HEADS, V_ROWS, per)


def _proj_x_kernel(x_ref, mod_ref, tab_ref, watt_ref, gq_ref, gkv_ref, wq_ref, wk_ref, wvt_ref,
                   q_ref, k_ref, vt_ref):
    mod = mod_ref[0]
    h = x_ref[0] * (1.0 + mod[1:2]) + mod[0:1]
    p = _dot_nt(h, watt_ref[...])
    tab = tab_ref[...]
    cq_t, sq_t, ck_t = tab[:, 0:LANES], tab[:, LANES:2 * LANES], tab[:, 2 * LANES:3 * LANES]
    sk_up, sk_down = tab[:, 3 * LANES:4 * LANES], tab[:, 4 * LANES:5 * LANES]
    kr0 = Q_LORA + KV_LORA
    kr = p[:, kr0:kr0 + LANES]
    quarter = QK_ROPE // 4
    kr_placed = (kr * ck_t + pltpu.roll(kr, quarter, 1) * sk_up
                 + pltpu.roll(kr, LANES - quarter, 1) * sk_down)
    _keys_values(p[:, Q_LORA:kr0], kr_placed, gkv_ref, wk_ref, wvt_ref, k_ref, vt_ref)
    qn = _rms_norm(p[:, 0:Q_LORA], gq_ref[...])
    qq = _dot(qn, wq_ref[...])
    width = MLA_HEADS * HEAD_PAD
    for hd in range(MLA_HEADS):
        a = qq[:, hd * HEAD_PAD:(hd + 1) * HEAD_PAD]
        b = qq[:, width + hd * HEAD_PAD:width + (hd + 1) * HEAD_PAD]
        q_ref[0, hd] = (a * cq_t + b * sq_t).astype(BF16)


def _proj_ctx_kernel(ctx_ref, mod_ref, watt_ref, gkv_ref, wk_ref, wvt_ref, k_ref, vt_ref):
    mod = mod_ref[0]
    ctx = ctx_ref[...].reshape(-1, ctx_ref.shape[2])
    h = ctx * (1.0 + mod[1:2]) + mod[0:1]
    p = _dot_nt(h, watt_ref[Q_LORA:Q_LORA + KV_LORA + LANES, :])
    _keys_values(p[:, 0:KV_LORA], p[:, KV_LORA:], gkv_ref, wk_ref, wvt_ref, k_ref, vt_ref)


def _proj_x(x, mods, tab, watt, gq, gkv, wq, wk, wvt):
    bsz, n, d = x.shape
    assert n % PROJ_TILE == 0
    const = lambda j, b: (0, 0)
    one = pl.Buffered(1)
    head_block = pl.BlockSpec((1, MLA_HEADS, PROJ_TILE, HEAD_PAD), lambda j, b: (b, 0, j, 0))
    return pl.pallas_call(
        _proj_x_kernel,
        grid=(n // PROJ_TILE, bsz),
        in_specs=[
            pl.BlockSpec((1, PROJ_TILE, d), lambda j, b: (b, j, 0)),
            pl.BlockSpec((1, SUBLANES, d), lambda j, b: (b, 0, 0)),
            pl.BlockSpec((PROJ_TILE, tab.shape[1]), lambda j, b: (j, 0)),
            pl.BlockSpec(watt.shape, const, pipeline_mode=one),
            pl.BlockSpec(gq.shape, const, pipeline_mode=one),
            pl.BlockSpec(gkv.shape, const, pipeline_mode=one),
            pl.BlockSpec(wq.shape, const, pipeline_mode=one),
            pl.BlockSpec(wk.shape, const, pipeline_mode=one),
            pl.BlockSpec(wvt.shape, const, pipeline_mode=one),
        ],
        out_specs=[head_block, head_block,
                   pl.BlockSpec((1, MLA_HEADS, V_ROWS, PROJ_TILE), lambda j, b: (b, 0, 0, j))],
        out_shape=[
            jax.ShapeDtypeStruct((bsz, MLA_HEADS, n, HEAD_PAD), BF16),
            jax.ShapeDtypeStruct((bsz, MLA_HEADS, n, HEAD_PAD), BF16),
            jax.ShapeDtypeStruct((bsz, MLA_HEADS, V_ROWS, n), BF16),
        ],
        compiler_params=pltpu.CompilerParams(
            dimension_semantics=("arbitrary", "arbitrary"), vmem_limit_bytes=VMEM_LIMIT),
        name="proj_x",
    )(x, mods, tab, watt, gq, gkv, wq, wk, wvt)


def _proj_ctx(ctx, mods, watt, gkv, wk, wvt):
    bsz, n_ctx, d = ctx.shape
    assert bsz % CTX_BATCH == 0
    const = lambda b: (0, 0)
    one = pl.Buffered(1)
    return pl.pallas_call(
        _proj_ctx_kernel,
        grid=(bsz // CTX_BATCH,),
        in_specs=[
            pl.BlockSpec((CTX_BATCH, n_ctx, d), lambda b: (b, 0, 0)),
            pl.BlockSpec((1, SUBLANES, d), lambda b: (bsz, 0, 0)),
            pl.BlockSpec(watt.shape, const, pipeline_mode=one),
            pl.BlockSpec(gkv.shape, const, pipeline_mode=one),
            pl.BlockSpec(wk.shape, const, pipeline_mode=one),
            pl.BlockSpec(wvt.shape, const, pipeline_mode=one),
        ],
        out_specs=[pl.BlockSpec((CTX_BATCH, MLA_HEADS, n_ctx, HEAD_PAD), lambda b: (b, 0, 0, 0)),
                   pl.BlockSpec((CTX_BATCH, MLA_HEADS, V_ROWS, n_ctx), lambda b: (b, 0, 0, 0))],
        out_shape=[
            jax.ShapeDtypeStruct((bsz, MLA_HEADS, n_ctx, HEAD_PAD), BF16),
            jax.ShapeDtypeStruct((bsz, MLA_HEADS, V_ROWS, n_ctx), BF16),
        ],
        compiler_params=pltpu.CompilerParams(dimension_semantics=("arbitrary",), vmem_limit_bytes=VMEM_LIMIT),
        name="proj_ctx",
    )(ctx, mods, watt, gkv, wk, wvt)


def _attn_kernel(q_ref, kx_ref, kc_ref, vtx_ref, vtc_ref, o_ref, s_ref):
    heads = q_ref.shape[1]
    nq = q_ref.shape[2] // Q_TILE
    nx = kx_ref.shape[2]

    def scores(h, i, slot):
        q = q_ref[0, h, i * Q_TILE:(i + 1) * Q_TILE, :]
        s_ref[slot, 0:nx] = _dot_nt(kx_ref[0, h], q)
        s_ref[slot, nx:] = _dot_nt(kc_ref[0, h], q)

    def finish(h, i, slot):
        s_t = s_ref[slot]
        m = jnp.max(s_t, axis=0, keepdims=True)
        p_t = jnp.exp2(s_t - m).astype(BF16)
        o_t = _dot(vtx_ref[0, h], p_t[0:nx]) + _dot(vtc_ref[0, h], p_t[nx:])
        o_ref[0, h, i] = o_t[0:V_DIM] * (1.0 / o_t[V_DIM:V_DIM + 1])

    def head(h, next_h):
        for i in range(nq):
            slot = i % 2
            if i + 1 < nq:
                scores(h, i + 1, 1 - slot)
            elif next_h is not None:
                scores(next_h, 0, 1 - slot)
            finish(h, i, slot)

    assert nq % 2 == 0
    scores(0, 0, 0)

    def body(h, carry):
        head(h, h + 1)
        return carry

    lax.fori_loop(0, heads - 1, body, 0)
    head(heads - 1, None)


def _attn(q, kx, kc, vtx, vtc):
    bsz, heads, n, _ = q.shape
    n_ctx = kc.shape[2]
    n_keys = n + n_ctx
    nq = n // Q_TILE
    whole = lambda b: (b, 0, 0, 0)
    return pl.pallas_call(
        _attn_kernel,
        grid=(bsz,),
        in_specs=[
            pl.BlockSpec((1, heads, n, HEAD_PAD), whole),
            pl.BlockSpec((1, heads, n, HEAD_PAD), whole),
            pl.BlockSpec((1, heads, n_ctx, HEAD_PAD), whole),
            pl.BlockSpec((1, heads, V_ROWS, n), whole),
            pl.BlockSpec((1, heads, V_ROWS, n_ctx), whole),
        ],
        out_specs=pl.BlockSpec((1, heads, nq, V_DIM, Q_TILE), lambda b: (b, 0, 0, 0, 0)),
        out_shape=jax.ShapeDtypeStruct((bsz, heads, nq, V_DIM, Q_TILE), F32),
        scratch_shapes=[pltpu.VMEM((2, n_keys, Q_TILE), F32)],
        compiler_params=pltpu.CompilerParams(
            dimension_semantics=("arbitrary",), vmem_limit_bytes=VMEM_LIMIT),
        name="attn",
    )(q, kx, kc, vtx, vtc)


def _mixer_kernel(x_ref, xp_ref, xn_ref, mod_ref, att_ref, wt_ref, cw_ref, woc_ref, wom_ref, wo_ref, ln_ref,
                  o_ref, res_ref, *, tiles_per_seq):
    t = pl.program_id(0)
    n_tiles = pl.num_programs(0) - 1

    def layer_norm():
        res = res_ref[...]
        mu = jnp.mean(res, axis=-1, keepdims=True)
        dev = res - mu
        var = jnp.mean(dev * dev, axis=-1, keepdims=True)
        ln = ln_ref[...]
        o_ref[0] = dev * lax.rsqrt(var + LN_EPS) * ln[0:1] + ln[1:2]

    @pl.when(t == 0)
    def _():
        res_ref[...] = jnp.zeros_like(res_ref)

    @pl.when(t == n_tiles)
    def _():
        layer_norm()

    @pl.when(t < n_tiles)
    def _():
        layer_norm()
        j = t % tiles_per_seq
        d = x_ref.shape[2]
        x = x_ref[0]
        rows = x.shape[0]
        mod = mod_ref[0]
        shift = mod[0:1]
        scale1 = 1.0 + mod[1:2]
        gate = mod[2:3]
        hx = x * scale1 + shift

        def proj(h, off, width):
            return _dot_nt(h, wt_ref[off:off + width, :])

        xh = jnp.concatenate([xp_ref[0], xn_ref[0]], axis=0)
        hh = xh * scale1 + shift
        hx_ext = jnp.concatenate([hx, hh], axis=0)
        u_ext = proj(hx_ext, O_XC, D_CONV) * proj(hx_ext, O_CC, D_CONV)
        u, uh = u_ext[:rows], u_ext[rows:]
        prev_edge = uh[SUBLANES - 1:SUBLANES] * (j > 0).astype(F32)
        next_edge = uh[SUBLANES:SUBLANES + 1] * (j < tiles_per_seq - 1).astype(F32)

        s_conv = jax.nn.sigmoid(proj(hx, O_GCONV, d))
        s_mla = jax.nn.sigmoid(proj(hx, O_GCONV + d, d))
        gm = _silu(proj(hx, O_GM, D_MLA))

        r = lax.broadcasted_iota(jnp.int32, u.shape, 0)
        u_prev = jnp.where(r == 0, prev_edge, pltpu.roll(u, 1, 0))
        u_next = jnp.where(r == rows - 1, next_edge, pltpu.roll(u, rows - 1, 0))
        cw = cw_ref[...]
        conv = u_prev * cw[0:1] + u * cw[1:2] + u_next * cw[2:3]

        zc = _silu(proj(hx, O_GC, D_CONV)) * proj(hx, O_BC, D_CONV) * conv
        yc = _dot(zc, woc_ref[...])

        att = jnp.concatenate([att_ref[0, :, c].reshape(D_MLA, Q_TILE).T
                               for c in range(att_ref.shape[2])], axis=0)
        ym = _dot(gm * att, wom_ref[...])

        merged = s_conv * yc + s_mla * ym
        y = _dot(merged, wo_ref[...])
        res_ref[...] = DEEPNORM_ALPHA * x + gate * y


def _mixer(x, mods, att_t, wt, cw, woc, wom, wo, ln):
    bsz, n, d = x.shape
    assert MIX_TILE % Q_TILE == 0 and n % MIX_TILE == 0
    nt = n // MIX_TILE
    n_tiles = bsz * nt
    blocks8 = MIX_TILE // SUBLANES
    const = lambda t: (0, 0)
    one = pl.Buffered(1)

    def tile(t):
        t = jnp.minimum(t, n_tiles - 1)
        return t // nt, t % nt

    def x_map(t):
        b, j = tile(t)
        return b, j, 0

    def prev_map(t):
        b, j = tile(t)
        return b, jnp.maximum(j * blocks8 - 1, 0), 0

    def next_map(t):
        b, j = tile(t)
        return b, jnp.minimum((j + 1) * blocks8, n // SUBLANES - 1), 0

    def mod_map(t):
        return tile(t)[0], 0, 0

    def att_map(t):
        b, j = tile(t)
        return b, 0, j, 0, 0

    def out_map(t):
        t = jnp.maximum(t - 1, 0)
        return t // nt, t % nt, 0

    return pl.pallas_call(
        functools.partial(_mixer_kernel, tiles_per_seq=nt),
        grid=(n_tiles + 1,),
        in_specs=[
            pl.BlockSpec((1, MIX_TILE, d), x_map),
            pl.BlockSpec((1, SUBLANES, d), prev_map),
            pl.BlockSpec((1, SUBLANES, d), next_map),
            pl.BlockSpec((1, SUBLANES, d), mod_map),
            pl.BlockSpec((1, MLA_HEADS, MIX_TILE // Q_TILE, V_DIM, Q_TILE), att_map),
            pl.BlockSpec(wt.shape, const, pipeline_mode=one),
            pl.BlockSpec(cw.shape, const, pipeline_mode=one),
            pl.BlockSpec(woc.shape, const, pipeline_mode=one),
            pl.BlockSpec(wom.shape, const, pipeline_mode=one),
            pl.BlockSpec(wo.shape, const, pipeline_mode=one),
            pl.BlockSpec(ln.shape, const, pipeline_mode=one),
        ],
        out_specs=pl.BlockSpec((1, MIX_TILE, d), out_map),
        out_shape=jax.ShapeDtypeStruct((bsz, n, d), F32),
        scratch_shapes=[pltpu.VMEM((MIX_TILE, d), F32)],
        compiler_params=pltpu.CompilerParams(dimension_semantics=("arbitrary",), vmem_limit_bytes=VMEM_LIMIT),
        name="mixer",
    )(x, x, x, mods, att_t, wt, cw, woc, wom, wo, ln)


def _rope_pair_swap():
    q = QK_ROPE // 4
    idx = np.concatenate([np.arange(q, 2 * q), np.arange(0, q), np.arange(3 * q, 4 * q), np.arange(2 * q, 3 * q)])
    sign = np.concatenate([-np.ones(q), np.ones(q), -np.ones(q), np.ones(q)]).astype(np.float32)
    return idx, sign


def _rope_tables(n, scale):
    rows = n // GRID_W
    row_pos = np.repeat(np.arange(rows), GRID_W).astype(np.float64)
    col_pos = np.tile(np.arange(GRID_W), rows).astype(np.float64)
    axis_dim = QK_ROPE // 2
    inv_freq = ROPE_THETA ** (-np.arange(0, axis_dim, 2, dtype=np.float64) / axis_dim)
    ang_r = row_pos[:, None] * inv_freq[None, :]
    ang_c = col_pos[:, None] * inv_freq[None, :]
    cos = np.concatenate([np.cos(ang_r), np.cos(ang_r), np.cos(ang_c), np.cos(ang_c)], axis=-1)
    sin = np.concatenate([np.sin(ang_r), np.sin(ang_r), np.sin(ang_c), np.sin(ang_c)], axis=-1)
    pad = np.zeros((n, HEAD_PAD - QK_NOPE - QK_ROPE))
    zero_n = np.zeros((n, QK_NOPE))
    cq = np.concatenate([np.full((n, QK_NOPE), scale), cos * scale, pad], axis=-1)
    sq = np.concatenate([zero_n, sin * scale, pad], axis=-1)
    ck = np.concatenate([zero_n, cos, pad], axis=-1)
    quarter = QK_ROPE // 4
    upper = (np.arange(QK_ROPE) // quarter) % 2 == 1
    sk_up = np.concatenate([zero_n, np.where(upper, sin, 0.0), pad], axis=-1)
    sk_down = np.concatenate([zero_n, np.where(upper, 0.0, -sin), pad], axis=-1)
    return jnp.asarray(np.concatenate([cq, sq, ck, sk_up, sk_down], axis=-1), dtype=F32)


def _place_rope_rows(w):
    d = w.shape[1]
    return jnp.concatenate([jnp.zeros((QK_NOPE, d), w.dtype), w,
                            jnp.zeros((HEAD_PAD - QK_NOPE - QK_ROPE, d), w.dtype)], axis=0)


def kernel(x, c, ctx, c_ctx, w_ada, b_ada, w_in, conv_w, q_norm_g, w_uq, kv_norm_g, w_ukv, w_out_conv,
           w_out_mla, w_o, ln_g, ln_b):
    bsz, n, d = x.shape
    assert w_ada.shape[0] == DEPTH == 1
    w_ada, b_ada, w_in, conv_w = w_ada[0], b_ada[0], w_in[0], conv_w[0]
    q_norm_g, w_uq, kv_norm_g, w_ukv = q_norm_g[0], w_uq[0], kv_norm_g[0], w_ukv[0]
    w_out_conv, w_out_mla, w_o, ln_g, ln_b = w_out_conv[0], w_out_mla[0], w_o[0], ln_g[0], ln_b[0]

    pad_rows = (-(bsz + 1)) % SUBLANES
    c_rows = jnp.concatenate([c, c_ctx[None, :], jnp.zeros((pad_rows, d), F32)], axis=0)
    mod = _adaln(c_rows, w_ada, b_ada[None, :])
    mods = mod[:bsz + 1].reshape(bsz + 1, 3, d)
    mods = jnp.concatenate([mods, jnp.zeros((bsz + 1, SUBLANES - 3, d), F32)], axis=1)

    wt = jnp.swapaxes(w_in, 0, 1)

    idx, sign = _rope_pair_swap()
    watt = jnp.concatenate([wt[O_CQ:O_KR], _place_rope_rows(wt[O_KR:O_KR + QK_ROPE])], axis=0)

    hq = QK_NOPE + QK_ROPE
    w_uq_h = w_uq.reshape(Q_LORA, MLA_HEADS, hq)
    zq = jnp.zeros((Q_LORA, MLA_HEADS, HEAD_PAD - hq), F32)
    wq_plain = jnp.concatenate([w_uq_h, zq], axis=-1).reshape(Q_LORA, MLA_HEADS * HEAD_PAD)
    wq_rot = jnp.concatenate([jnp.zeros((Q_LORA, MLA_HEADS, QK_NOPE), F32),
                              w_uq_h[..., QK_NOPE + idx] * sign, zq], axis=-1).reshape(Q_LORA, MLA_HEADS * HEAD_PAD)
    wq = jnp.concatenate([wq_plain, wq_rot], axis=-1)

    w_ukv_h = w_ukv.reshape(KV_LORA, MLA_HEADS, QK_NOPE + V_DIM)
    wk = jnp.concatenate([w_ukv_h[..., :QK_NOPE], jnp.zeros((KV_LORA, MLA_HEADS, HEAD_PAD - QK_NOPE), F32)],
                         axis=-1).reshape(KV_LORA, MLA_HEADS * HEAD_PAD)
    wv = jnp.concatenate([w_ukv_h[..., QK_NOPE:], jnp.zeros((KV_LORA, MLA_HEADS, V_ROWS - V_DIM), F32)], axis=-1)
    wvt = wv.reshape(KV_LORA, MLA_HEADS * V_ROWS).T

    tab = _rope_tables(n, float((QK_NOPE + QK_ROPE) ** -0.5 * np.log2(np.e)))
    gq, gkv = q_norm_g[None, :], kv_norm_g[None, :]
    q, kx, vtx = _proj_x(x, mods, tab, watt, gq, gkv, wq, wk, wvt)
    kc, vtc = _proj_ctx(ctx, mods, watt, gkv, wk, wvt)
    att_t = _attn(q, kx, kc, vtx, vtc)

    cw = jnp.concatenate([conv_w, jnp.zeros((SUBLANES - conv_w.shape[0], D_CONV), F32)], axis=0)
    ln = jnp.concatenate([ln_g[None, :], ln_b[None, :], jnp.zeros((SUBLANES - 2, d), F32)], axis=0)
    return _mixer(x, mods, att_t, wt, cw, w_out_conv, w_out_mla, w_o, ln)
```
